```python
import math
import jax, jax.numpy as jnp
from jax import lax
import numpy as np

D_MODEL = 2048
BATCH = 8
SEQ = 2048
DEPTH = 2

CTX_LEN = 256
GRID_W = 64
N_MOD = 6
EPS = 1e-6

GDN_HEADS = 12
GDN_DK = 128
GDN_DV = 128
GDN_W = GDN_HEADS * GDN_DV
GDN_CONV = 5
GDN_CHUNK = 64
FNET_GROUPS = 4
FNET_GW = 128
FNET_W = FNET_GROUPS * FNET_GW
EVEN_MIX_W = GDN_W + FNET_W
EVEN_IN_W = 4 * GDN_W + 4 * GDN_HEADS + FNET_W

SWA_HEADS = 32
SWA_KV = 4
SWA_GROUP = SWA_HEADS // SWA_KV
SWA_DH = 64
SWA_WINDOW = 128
SWA_BLOCK = 128
ROPE_BASE = 10000.0
SWA_Q_W = SWA_HEADS * SWA_DH
SWA_KV_W = SWA_KV * SWA_DH
SWA_IN_W = SWA_Q_W + 2 * SWA_KV_W

PEER_HEADS = 8
PEER_NKEYS = 128
PEER_EXPERTS = PEER_NKEYS * PEER_NKEYS
PEER_QDIM = 256
PEER_HALF = PEER_QDIM // 2
PEER_TOPK = 16
PEER_BLOCK = 128

kernel_name = 'hybrid_gdn_fnet_swa_peer_dit'


def rms_norm(x, gain):
    xf = x.astype(jnp.float32)
    y = xf * lax.rsqrt(jnp.mean(xf * xf, axis=-1, keepdims=True) + EPS)
    return (y * gain.astype(jnp.float32)).astype(x.dtype)


def modulate(x, gain, shift, scale):
    return rms_norm(x, gain) * (1 + scale) + shift


def ada_mod(cond, w, b):
    return jnp.split(jax.nn.silu(cond) @ w + b, N_MOD, axis=-1)


def l2norm(t):
    return t * lax.rsqrt(jnp.sum(t * t, axis=-1, keepdims=True) + 1e-6)


def short_conv(x, w):
    pad = (GDN_CONV - 1) // 2
    y = lax.conv_general_dilated(x, w[:, None, :].astype(x.dtype), (1,), [(pad, pad)],
                                 dimension_numbers=('NWC', 'WIO', 'NWC'),
                                 feature_group_count=x.shape[-1])
    return jax.nn.silu(y)


def gdn_inputs(p, conv_w, a_log, dt_bias):
    Bn, L, _ = p.shape
    qkv = short_conv(p[..., :3 * GDN_W], conv_w).astype(jnp.float32)
    q, k, v = jnp.split(qkv, 3, axis=-1)
    q = l2norm(q.reshape(Bn, L, GDN_HEADS, GDN_DK))
    k = l2norm(k.reshape(Bn, L, GDN_HEADS, GDN_DK))
    v = v.reshape(Bn, L, GDN_HEADS, GDN_DV)
    o = 4 * GDN_W
    ba = p[..., o:o + 4 * GDN_HEADS].astype(jnp.float32).reshape(Bn, L, 2, 2, GDN_HEADS)
    beta = jax.nn.sigmoid(ba[:, :, 0])
    g = -jnp.exp(a_log.astype(jnp.float32)) * jax.nn.softplus(ba[:, :, 1] + dt_bias.astype(jnp.float32))
    return q, k, v, beta, g


def gated_delta_dir(q, k, v, g, beta, s0):
    Bn, L, H, DK = q.shape
    N = L // GDN_CHUNK

    def chunks(t):
        return jnp.moveaxis(t.reshape(Bn, N, GDN_CHUNK, H, t.shape[-1]), 3, 1)

    def chunks1(t):
        return jnp.moveaxis(t.reshape(Bn, N, GDN_CHUNK, H), 3, 1)

    q = chunks(q) * (DK ** -0.5)
    k = chunks(k)
    v = chunks(v)
    gc = jnp.cumsum(chunks1(g), axis=-1)
    beta = chunks1(beta)
    idx = jnp.arange(GDN_CHUNK)
    incl = idx[:, None] >= idx[None, :]
    strict = idx[:, None] > idx[None, :]
    decay = jnp.exp(jnp.where(incl, gc[..., :, None] - gc[..., None, :], -jnp.inf))
    kb = k * beta[..., None]
    vb = v * beta[..., None]
    lmat = jnp.where(strict, jnp.einsum('bhnid,bhnjd->bhnij', kb, k) * decay, 0.0)
    eye = jnp.eye(GDN_CHUNK, dtype=jnp.float32)
    tinv = lax.linalg.triangular_solve(eye + lmat, jnp.broadcast_to(eye, lmat.shape),
                                       left_side=True, lower=True, unit_diagonal=True)
    u = tinv @ vb
    w = tinv @ (kb * jnp.exp(gc)[..., None])
    attn = jnp.where(incl, jnp.einsum('bhnid,bhnjd->bhnij', q, k) * decay, 0.0)
    qg = q * jnp.exp(gc)[..., None]
    kd = k * jnp.exp(gc[..., -1:] - gc)[..., None]
    glast = jnp.exp(gc[..., -1])

    def step(S, xs):
        u_i, w_i, attn_i, qg_i, kd_i, gl_i = xs
        v_new = u_i - w_i @ S
        o_i = qg_i @ S + attn_i @ v_new
        S = S * gl_i[..., None, None] + jnp.swapaxes(kd_i, -1, -2) @ v_new
        return S, o_i

    xs = tuple(jnp.moveaxis(t, 2, 0) for t in (u, w, attn, qg, kd, glast))
    S, o = lax.scan(step, s0, xs)
    o = jnp.transpose(o, (1, 0, 3, 2, 4)).reshape(Bn, L, H, GDN_DV)
    return o, S


def gdn_bidir(inputs, s0_f, s0_b):
    q, k, v, beta, g = inputs
    o_f, s_f = gated_delta_dir(q, k, v, g[:, :, 0], beta[:, :, 0], s0_f)
    fl = lambda t: jnp.flip(t, axis=1)
    o_b, s_b = gated_delta_dir(fl(q), fl(k), fl(v), fl(g[:, :, 1]), fl(beta[:, :, 1]), s0_b)
    return o_f + fl(o_b), s_f, s_b


def gated_out(o, z, w):
    Bn, L = z.shape[:2]
    zf = z.astype(jnp.float32).reshape(Bn, L, GDN_HEADS, GDN_DV)
    y = o * lax.rsqrt(jnp.mean(o * o, axis=-1, keepdims=True) + EPS) * w.astype(jnp.float32) * jax.nn.silu(zf)
    return y.reshape(Bn, L, GDN_W).astype(z.dtype)


def fourier_mix(f):
    Bn, L, _ = f.shape
    fg = f.astype(jnp.float32).reshape(Bn, L, FNET_GROUPS, FNET_GW).transpose(0, 2, 1, 3)
    y = jnp.real(jnp.fft.fft2(fg, norm='ortho'))
    return y.transpose(0, 2, 1, 3).reshape(Bn, L, FNET_W).astype(f.dtype)


def even_mixer(h_lat, h_ctx, w_in, conv_w, a_log, dt_bias, out_norm, w_out, ctx_out):
    p_lat = h_lat @ w_in
    p_ctx = h_ctx @ w_in
    zero = jnp.zeros((h_lat.shape[0], GDN_HEADS, GDN_DK, GDN_DV), jnp.float32)
    o_ctx, s_f, s_b = gdn_bidir(gdn_inputs(p_ctx, conv_w, a_log, dt_bias), zero, zero)
    o_lat, _, _ = gdn_bidir(gdn_inputs(p_lat, conv_w, a_log, dt_bias), s_f, s_b)

    def finish(p, o):
        y = gated_out(o, p[..., 3 * GDN_W:4 * GDN_W], out_norm)
        f = fourier_mix(p[..., 4 * GDN_W + 4 * GDN_HEADS:])
        return jnp.concatenate([y, f], axis=-1) @ w_out

    y_lat = finish(p_lat, o_lat)
    y_ctx = finish(p_ctx, o_ctx) if ctx_out else None
    return y_lat, y_ctx


def axial_rope_tables(L, dtype):
    rows = L // GRID_W
    r = jnp.broadcast_to(jnp.arange(rows)[:, None], (rows, GRID_W)).reshape(-1).astype(jnp.float32)
    col = jnp.broadcast_to(jnp.arange(GRID_W)[None, :], (rows, GRID_W)).reshape(-1).astype(jnp.float32)
    n = SWA_DH // 4
    freq = ROPE_BASE ** (-jnp.arange(n, dtype=jnp.float32) / n)
    ang = jnp.concatenate([r[:, None] * freq, col[:, None] * freq], axis=-1)
    return jnp.cos(ang).astype(dtype), jnp.sin(ang).astype(dtype)


def apply_axial_rope(x, cos, sin):
    n = SWA_DH // 4
    shape = (x.shape[1],) + (1,) * (x.ndim - 3) + (cos.shape[-1],)
    cos = cos.reshape(shape)
    sin = sin.reshape(shape)

    def rot(t, cs, sn):
        t1, t2 = t[..., :n], t[..., n:]
        return jnp.concatenate([t1 * cs - t2 * sn, t2 * cs + t1 * sn], axis=-1)

    half = SWA_DH // 2
    return jnp.concatenate([rot(x[..., :half], cos[..., :n], sin[..., :n]),
                            rot(x[..., half:], cos[..., n:], sin[..., n:])], axis=-1)


def sink_attention(q, keys, vals, masks, sink):
    Bn, Q = q.shape[:2]
    scale = SWA_DH ** -0.5
    logits = [jnp.broadcast_to(sink[None, :, :, None, None], (Bn, SWA_KV, SWA_GROUP, Q, 1))]
    for k, m in zip(keys, masks):
        s = jnp.einsum('bqhgd,bkhd->bhgqk', q, k).astype(jnp.float32) * scale
        logits.append(s if m is None else jnp.where(m, s, -jnp.inf))
    p = jax.nn.softmax(jnp.concatenate(logits, axis=-1), axis=-1)
    out, off = 0, 1
    for v in vals:
        n = v.shape[1]
        out = out + jnp.einsum('bhgqk,bkhd->bqhgd', p[..., off:off + n].astype(v.dtype), v)
        off += n
    return out


def odd_mixer(h_lat, h_ctx, w_qkv, sinks, w_out, ctx_out):
    Bn, L, _ = h_lat.shape
    C = h_ctx.shape[1]
    sink = sinks.astype(jnp.float32).reshape(SWA_KV, SWA_GROUP)
    p_ckv = h_ctx @ w_qkv[:, SWA_Q_W:]
    k_c = p_ckv[..., :SWA_KV_W].reshape(Bn, C, SWA_KV, SWA_DH)
    v_c = p_ckv[..., SWA_KV_W:].reshape(Bn, C, SWA_KV, SWA_DH)
    p = h_lat @ w_qkv
    q = p[..., :SWA_Q_W].reshape(Bn, L, SWA_KV, SWA_GROUP, SWA_DH)
    k = p[..., SWA_Q_W:SWA_Q_W + SWA_KV_W].reshape(Bn, L, SWA_KV, SWA_DH)
    v = p[..., SWA_Q_W + SWA_KV_W:].reshape(Bn, L, SWA_KV, SWA_DH)
    cos, sin = axial_rope_tables(L, h_lat.dtype)
    q = apply_axial_rope(q, cos, sin)
    k = apply_axial_rope(k, cos, sin)
    pads = ((0, 0), (SWA_WINDOW, SWA_WINDOW), (0, 0), (0, 0))
    kp = jnp.pad(k, pads)
    vp = jnp.pad(v, pads)
    span = SWA_BLOCK + 2 * SWA_WINDOW

    def block(b):
        start = b * SWA_BLOCK
        qb = lax.dynamic_slice_in_dim(q, start, SWA_BLOCK, axis=1)
        kb = lax.dynamic_slice_in_dim(kp, start, span, axis=1)
        vb = lax.dynamic_slice_in_dim(vp, start, span, axis=1)
        qpos = start + jnp.arange(SWA_BLOCK)
        kpos = start - SWA_WINDOW + jnp.arange(span)
        band = ((jnp.abs(qpos[:, None] - kpos[None, :]) <= SWA_WINDOW)
                & (kpos[None, :] >= 0) & (kpos[None, :] < L))
        return sink_attention(qb, [k_c, kb], [v_c, vb], [None, band], sink)

    o = lax.map(block, jnp.arange(L // SWA_BLOCK))
    y_lat = jnp.moveaxis(o, 0, 1).reshape(Bn, L, SWA_Q_W) @ w_out
    y_ctx = None
    if ctx_out:
        q_c = (h_ctx @ w_qkv[:, :SWA_Q_W]).reshape(Bn, C, SWA_KV, SWA_GROUP, SWA_DH)
        y_ctx = sink_attention(q_c, [k_c], [v_c], [None], sink).reshape(Bn, C, SWA_Q_W) @ w_out
    return y_lat, y_ctx


def peer_ffn(h, w_q, keys, u, v):
    Bn, L, D = h.shape
    q = (h @ w_q).astype(jnp.float32).reshape(Bn, L, PEER_HEADS, 2, PEER_HALF)
    s = jnp.einsum('blhpd,pnd->blhpn', q, keys.astype(jnp.float32))
    s_top, i_top = lax.top_k(s, PEER_TOPK)
    cand = (s_top[..., 0, :, None] + s_top[..., 1, None, :]).reshape(Bn, L, PEER_HEADS, PEER_TOPK * PEER_TOPK)
    cidx = (i_top[..., 0, :, None] * PEER_NKEYS + i_top[..., 1, None, :]).reshape(Bn, L, PEER_HEADS, PEER_TOPK * PEER_TOPK)
    best, pos = lax.top_k(cand, PEER_TOPK)
    idx = jnp.take_along_axis(cidx, pos, axis=-1)
    gate = jax.nn.softmax(best, axis=-1)
    nblk = Bn * L // PEER_BLOCK
    hb = h.reshape(nblk, PEER_BLOCK, D)
    ib = idx.reshape(nblk, PEER_BLOCK, PEER_HEADS * PEER_TOPK)
    gb = gate.reshape(nblk, PEER_BLOCK, PEER_HEADS * PEER_TOPK).astype(h.dtype)

    def block(args):
        xt, it, gt = args
        a = jax.nn.gelu(jnp.einsum('td,tkd->tk', xt, u[it]), approximate=False)
        return jnp.einsum('tk,tkd->td', gt * a, v[it])

    return lax.map(block, (hb, ib, gb)).reshape(Bn, L, D)


def setup_inputs(seed: int = 0) -> dict:
    key = jax.random.key(seed)
    ks = jax.random.split(key, 24)
    f32 = jnp.float32
    n_even = (DEPTH + 1) // 2
    n_odd = DEPTH // 2

    def nrm(k, shape, s):
        return jax.random.normal(k, shape, f32) * s

    dt = jnp.exp(jax.random.uniform(ks[11], (n_even, 2, GDN_HEADS), f32, math.log(1e-3), math.log(1e-1)))
    return {
        'x': nrm(ks[0], (BATCH, SEQ, D_MODEL), 1.0),
        'c': nrm(ks[1], (BATCH, D_MODEL), 1.0),
        'ctx': nrm(ks[2], (BATCH, CTX_LEN, D_MODEL), 1.0),
        'c_ctx': nrm(ks[3], (D_MODEL,), 1.0),
        'ada_w': nrm(ks[4], (DEPTH, D_MODEL, N_MOD * D_MODEL), 0.5 * D_MODEL ** -0.5),
        'ada_b': nrm(ks[5], (DEPTH, N_MOD * D_MODEL), 0.02),
        'norm_mix': 1.0 + nrm(ks[6], (DEPTH, D_MODEL), 0.02),
        'norm_ffn': 1.0 + nrm(ks[7], (DEPTH, D_MODEL), 0.02),
        'even_w_in': nrm(ks[8], (n_even, D_MODEL, EVEN_IN_W), D_MODEL ** -0.5),
        'gdn_conv': nrm(ks[9], (n_even, GDN_CONV, 3 * GDN_W), GDN_CONV ** -0.5),
        'gdn_a_log': jnp.log(jax.random.uniform(ks[10], (n_even, 2, GDN_HEADS), f32, 1.0, 16.0)),
        'gdn_dt_bias': dt + jnp.log(-jnp.expm1(-dt)),
        'gdn_out_norm': 1.0 + nrm(ks[12], (n_even, GDN_DV), 0.02),
        'even_w_out': nrm(ks[13], (n_even, EVEN_MIX_W, D_MODEL), EVEN_MIX_W ** -0.5),
        'odd_w_qkv': nrm(ks[14], (n_odd, D_MODEL, SWA_IN_W), D_MODEL ** -0.5),
        'odd_sinks': nrm(ks[15], (n_odd, SWA_HEADS), 1.0),
        'odd_w_out': nrm(ks[16], (n_odd, SWA_Q_W, D_MODEL), SWA_Q_W ** -0.5),
        'peer_w_q': nrm(ks[17], (DEPTH, D_MODEL, PEER_HEADS * PEER_QDIM), D_MODEL ** -0.5),
        'peer_keys': nrm(ks[18], (DEPTH, 2, PEER_NKEYS, PEER_HALF), PEER_HALF ** -0.5),
        'peer_u': nrm(ks[19], (DEPTH, PEER_EXPERTS, D_MODEL), D_MODEL ** -0.5),
        'peer_v': nrm(ks[20], (DEPTH, PEER_EXPERTS, D_MODEL), 1.0),
        'final_norm': 1.0 + nrm(ks[21], (D_MODEL,), 0.02),
    }


def reference(x, c, ctx, c_ctx, ada_w, ada_b, norm_mix, norm_ffn, even_w_in, gdn_conv,
              gdn_a_log, gdn_dt_bias, gdn_out_norm, even_w_out, odd_w_qkv, odd_sinks,
              odd_w_out, peer_w_q, peer_keys, peer_u, peer_v, final_norm):
    h_c = ctx
    for i in range(DEPTH):
        last = i == DEPTH - 1
        j = i // 2
        m_lat = [t[:, None, :] for t in ada_mod(c, ada_w[i], ada_b[i])]
        m_ctx = ada_mod(c_ctx, ada_w[i], ada_b[i])
        a_lat = modulate(x, norm_mix[i], m_lat[0], m_lat[1])
        a_ctx = modulate(h_c, norm_mix[i], m_ctx[0], m_ctx[1])
        if i % 2 == 0:
            y_lat, y_ctx = even_mixer(a_lat, a_ctx, even_w_in[j], gdn_conv[j], gdn_a_log[j],
                                      gdn_dt_bias[j], gdn_out_norm[j], even_w_out[j], not last)
        else:
            y_lat, y_ctx = odd_mixer(a_lat, a_ctx, odd_w_qkv[j], odd_sinks[j], odd_w_out[j], not last)
        x = x + m_lat[2] * y_lat
        x = x + m_lat[5] * peer_ffn(modulate(x, norm_ffn[i], m_lat[3], m_lat[4]),
                                    peer_w_q[i], peer_keys[i], peer_u[i], peer_v[i])
        if not last:
            h_c = h_c + m_ctx[2] * y_ctx
            h_c = h_c + m_ctx[5] * peer_ffn(modulate(h_c, norm_ffn[i], m_ctx[3], m_ctx[4]),
                                            peer_w_q[i], peer_keys[i], peer_u[i], peer_v[i])
    return rms_norm(x, final_norm)
```

```python
import functools
import math

import jax
import jax.numpy as jnp
from jax import lax
from jax.experimental import pallas as pl
from jax.experimental.pallas import tpu as pltpu

N_MOD = 6
EPS = 1e-6
GDN_HEADS = 12
GDN_DK = 128
GDN_DV = 128
GDN_W = GDN_HEADS * GDN_DV
GDN_CONV = 5
GDN_CHUNK = 64
FNET_GROUPS = 4
FNET_GW = 128
FNET_W = FNET_GROUPS * FNET_GW
GRID_W = 64
SWA_HEADS = 32
SWA_KV = 4
SWA_GROUP = SWA_HEADS // SWA_KV
SWA_DH = 64
SWA_WINDOW = 128
SWA_BLOCK = 128
ROPE_BASE = 10000.0
SWA_Q_W = SWA_HEADS * SWA_DH
SWA_KV_W = SWA_KV * SWA_DH
PEER_HEADS = 8
PEER_NKEYS = 128
PEER_HALF = 128
PEER_TOPK = 16

LANES = 128
VMEM_BYTES_V7X = 64 * 1024 * 1024
MXU_DTYPE = jnp.bfloat16

_HI = lax.Precision.HIGHEST
_NT = (((1,), (1,)), ((), ()))
_TN = (((0,), (0,)), ((), ()))


def _cparams(semantics, vmem_mb):
    assert vmem_mb * 1024 * 1024 < VMEM_BYTES_V7X
    return pltpu.CompilerParams(dimension_semantics=semantics,
                                vmem_limit_bytes=vmem_mb * 1024 * 1024)


def _mm(a, b):
    return jnp.dot(a.astype(MXU_DTYPE), b.astype(MXU_DTYPE),
                   preferred_element_type=jnp.float32)


def _mm_nt(a, b):
    return lax.dot_general(a.astype(MXU_DTYPE), b.astype(MXU_DTYPE), _NT,
                           preferred_element_type=jnp.float32)


def _mm_tn(a, b):
    return lax.dot_general(a.astype(MXU_DTYPE), b.astype(MXU_DTYPE), _TN,
                           preferred_element_type=jnp.float32)


def _mm_f32(a, b):
    return jnp.dot(a, b, precision=_HI, preferred_element_type=jnp.float32)


def _sigmoid(x):
    return 1.0 / (1.0 + jnp.exp(-x))


def _silu(x):
    return x * _sigmoid(x)


def _softplus(x):
    return jnp.maximum(x, 0.0) + jnp.log1p(jnp.exp(-jnp.abs(x)))


def _ada_kernel(cond_ref, w_ref, b_ref, o_ref):
    a = _silu(cond_ref[...])
    o_ref[0] = _mm(a, w_ref[0]) + b_ref[0]


def ada_mod_all(cond, ada_w, ada_b, *, tn=1024):
    depth, d, n = ada_w.shape
    r = cond.shape[0]
    return pl.pallas_call(
        _ada_kernel,
        grid=(depth, n // tn),
        in_specs=[
            pl.BlockSpec((r, d), lambda l, j: (0, 0)),
            pl.BlockSpec((1, d, tn), lambda l, j: (l, 0, j)),
            pl.BlockSpec((1, 1, tn), lambda l, j: (l, 0, j)),
        ],
        out_specs=pl.BlockSpec((1, r, tn), lambda l, j: (l, 0, j)),
        out_shape=jax.ShapeDtypeStruct((depth, r, n), jnp.float32),
        compiler_params=_cparams(("parallel", "parallel"), 40),
        name="ada_mod",
    )(cond, ada_w, ada_b.reshape(depth, 1, n))


def _rope_swap(y):
    n = y.shape[-1]
    lane = lax.broadcasted_iota(jnp.int32, y.shape, 1)
    up = pltpu.roll(y, n - 16, 1)
    down = pltpu.roll(y, 16, 1)
    return jnp.where(lane % 32 < 16, up, down)


def _mod_matmul_kernel(*refs, rope_tiles, emit_a, tn):
    x_ref, gain_ref, shift_ref, scale_ref, w_ref = refs[:5]
    pos = 5
    if rope_tiles:
        cos_ref, sin_ref = refs[pos:pos + 2]
        pos += 2
    o_ref = refs[pos]
    pos += 1
    if emit_a:
        a_out_ref = refs[pos]
        pos += 1
    a_ref = refs[pos]
    j = pl.program_id(2)

    @pl.when(j == 0)
    def _():
        x = x_ref[0]
        y = x * lax.rsqrt(jnp.mean(x * x, axis=-1, keepdims=True) + EPS)
        y = y * gain_ref[...]
        a = (y * (1.0 + scale_ref[0]) + shift_ref[0]).astype(a_ref.dtype)
        a_ref[...] = a
        if emit_a:
            a_out_ref[0] = a

    acc = jnp.dot(a_ref[...], w_ref[...], preferred_element_type=jnp.float32)
    if rope_tiles:
        @pl.when(j < rope_tiles)
        def _():
            cos = cos_ref[...]
            sin = sin_ref[...]
            for t in range(tn // LANES):
                sl = slice(t * LANES, (t + 1) * LANES)
                y = acc[:, sl]
                o_ref[0, :, sl] = (y * cos + _rope_swap(y) * sin).astype(o_ref.dtype)

        @pl.when(j >= rope_tiles)
        def _():
            o_ref[0] = acc.astype(o_ref.dtype)
    else:
        o_ref[0] = acc.astype(o_ref.dtype)


def mod_matmul(x, gain, shift, scale, w, *, tm, tn, rope=None, rope_tiles=0, emit_a=False):
    b, l, d = x.shape
    n = w.shape[1]
    tm = min(tm, l)
    assert l % tm == 0 and n % tn == 0 and tn % LANES == 0
    in_specs = [
        pl.BlockSpec((1, tm, d), lambda bi, i, j: (bi, i, 0)),
        pl.BlockSpec((1, d), lambda bi, i, j: (0, 0)),
        pl.BlockSpec((1, 1, d), lambda bi, i, j: (bi, 0, 0)),
        pl.BlockSpec((1, 1, d), lambda bi, i, j: (bi, 0, 0)),
        pl.BlockSpec((d, tn), lambda bi, i, j: (0, j)),
    ]
    args = [x, gain.reshape(1, d), shift.reshape(b, 1, d), scale.reshape(b, 1, d), w]
    if rope_tiles:
        in_specs += [pl.BlockSpec((tm, LANES), lambda bi, i, j: (i, 0))] * 2
        args += list(rope)
    out_specs = [pl.BlockSpec((1, tm, tn), lambda bi, i, j: (bi, i, j))]
    out_shape = [jax.ShapeDtypeStruct((b, l, n), jnp.float32)]
    if emit_a:
        out_specs.append(pl.BlockSpec((1, tm, d), lambda bi, i, j: (bi, i, 0)))
        out_shape.append(jax.ShapeDtypeStruct((b, l, d), w.dtype))
    res = pl.pallas_call(
        functools.partial(_mod_matmul_kernel, rope_tiles=rope_tiles, emit_a=emit_a, tn=tn),
        grid=(b, l // tm, n // tn),
        in_specs=in_specs,
        out_specs=out_specs,
        out_shape=out_shape,
        scratch_shapes=[pltpu.VMEM((tm, d), w.dtype)],
        compiler_params=_cparams(("parallel", "parallel", "arbitrary"), 48),
        name="mod_matmul",
    )(*args)
    return res if emit_a else res[0]


def _resid_matmul_kernel(*refs, n_pairs):
    a_refs = refs[:n_pairs]
    w_refs = refs[n_pairs:2 * n_pairs]
    x_ref, gate_ref, o_ref = refs[2 * n_pairs:]
    acc = jnp.dot(a_refs[0][0], w_refs[0][...], preferred_element_type=jnp.float32)
    for a_ref, w_ref in zip(a_refs[1:], w_refs[1:]):
        acc += jnp.dot(a_ref[0], w_ref[...], preferred_element_type=jnp.float32)
    o_ref[0] = x_ref[0] + gate_ref[0] * acc


def resid_matmul(acts, weights, x, gate, *, tm, tn):
    b, l, n = x.shape
    tm = min(tm, l)
    in_specs = []
    for a in acts:
        in_specs.append(pl.BlockSpec((1, tm, a.shape[2]), lambda bi, i, j: (bi, i, 0)))
    for w in weights:
        in_specs.append(pl.BlockSpec((w.shape[0], tn), lambda bi, i, j: (0, j)))
    in_specs += [
        pl.BlockSpec((1, tm, tn), lambda bi, i, j: (bi, i, j)),
        pl.BlockSpec((1, 1, tn), lambda bi, i, j: (bi, 0, j)),
    ]
    return pl.pallas_call(
        functools.partial(_resid_matmul_kernel, n_pairs=len(acts)),
        grid=(b, l // tm, n // tn),
        in_specs=in_specs,
        out_specs=pl.BlockSpec((1, tm, tn), lambda bi, i, j: (bi, i, j)),
        out_shape=jax.ShapeDtypeStruct((b, l, n), jnp.float32),
        compiler_params=_cparams(("parallel", "parallel", "parallel"), 40),
        name="resid_matmul",
    )(*acts, *weights, x, gate.reshape(b, 1, n))


def _gdn_conv_kernel(p_ref, w_ref, o_ref):
    c = pl.program_id(1)
    x = p_ref[0]
    l = x.shape[0]
    row = lax.broadcasted_iota(jnp.int32, x.shape, 0)
    pad = (GDN_CONV - 1) // 2
    y = x * w_ref[0, pad:pad + 1, :]
    for t in range(GDN_CONV):
        s = t - pad
        if s == 0:
            continue
        xs = pltpu.roll(x, (-s) % l, 0)
        ok = (row + s >= 0) & (row + s < l)
        y = y + jnp.where(ok, xs, 0.0) * w_ref[0, t:t + 1, :]
    y = _silu(y)
    inv = lax.rsqrt(jnp.sum(y * y, axis=-1, keepdims=True) + 1e-6)
    fac = jnp.where(c < 2 * GDN_HEADS, inv, 1.0)
    fac = fac * jnp.where(c < GDN_HEADS, GDN_DK ** -0.5, 1.0)
    o_ref[0] = y * fac


def gdn_short_conv(p, conv_w):
    b, l, _ = p.shape
    nt = 3 * GDN_W // LANES
    w = conv_w.reshape(GDN_CONV, nt, LANES).transpose(1, 0, 2)
    return pl.pallas_call(
        _gdn_conv_kernel,
        grid=(b, nt),
        in_specs=[
            pl.BlockSpec((1, l, LANES), lambda bi, c: (bi, 0, c)),
            pl.BlockSpec((1, GDN_CONV, LANES), lambda bi, c: (c, 0, 0)),
        ],
        out_specs=pl.BlockSpec((1, l, LANES), lambda bi, c: (bi, 0, c)),
        out_shape=jax.ShapeDtypeStruct((b, l, 3 * GDN_W), jnp.float32),
        compiler_params=_cparams(("parallel", "parallel"), 32),
        name="gdn_conv",
    )(p, w)


def _unit_lower_inverse(lm):
    c = lm.shape[0]
    eye = (lax.broadcasted_iota(jnp.int32, (c, c), 0)
           == lax.broadcasted_iota(jnp.int32, (c, c), 1)).astype(jnp.float32)
    p = -lm
    t = eye + p
    for _ in range(int(math.log2(c)) - 1):
        p = _mm_f32(p, p)
        t = t + _mm_f32(t, p)
    return t


def _gdn_chunk_kernel(q_ref, k_ref, v_ref, ba_ref, a_ref, dt_ref, s0_ref,
                      o_ref, sout_ref, s_ref):
    d = pl.program_id(0)
    n = pl.program_id(2)
    c = GDN_CHUNK

    @pl.when(n == 0)
    def _():
        s_ref[...] = s0_ref[0, 0]

    ba = ba_ref[0]
    beta_all = _sigmoid(ba)
    g_all = -a_ref[0] * _softplus(ba + dt_ref[0])
    ii = lax.broadcasted_iota(jnp.int32, (c, c), 0)
    jj = lax.broadcasted_iota(jnp.int32, (c, c), 1)
    sgn = 1 - 2 * d
    diff = (ii - jj) * sgn
    incl = diff >= 0
    strict = diff > 0
    gc_all = _mm_f32(incl.astype(jnp.float32), g_all)
    gtot_all = jnp.sum(g_all, axis=0, keepdims=True)
    gc_all_t = gc_all.T
    eg_all = jnp.exp(gc_all)
    er_all = jnp.exp(gtot_all - gc_all)
    el_all = jnp.exp(gtot_all)

    for h in range(GDN_HEADS):
        sl = slice(h * GDN_DK, (h + 1) * GDN_DK)
        col = GDN_HEADS + h
        q = q_ref[0, :, sl]
        k = k_ref[0, :, sl]
        v = v_ref[0, :, sl]
        beta = beta_all[:, h:h + 1]
        gc_col = gc_all[:, col:col + 1]
        gc_row = gc_all_t[col:col + 1, :]
        eg = eg_all[:, col:col + 1]
        dec = jnp.exp(jnp.where(incl, gc_col - gc_row, -jnp.inf))
        kb = k * beta
        vb = v * beta
        lm = jnp.where(strict, _mm_nt(kb, k) * dec, 0.0)
        tinv = _unit_lower_inverse(lm)
        u = _mm(tinv, vb)
        w = _mm(tinv, kb * eg)
        attn = jnp.where(incl, _mm_nt(q, k) * dec, 0.0)
        qg = q * eg
        kd = k * er_all[:, col:col + 1]
        s = s_ref[h]
        v_new = u - _mm(w, s)
        o_ref[0, 0, :, sl] = _mm(qg, s) + _mm(attn, v_new)
        s_ref[h] = s * el_all[:, col:col + 1] + _mm_tn(kd, v_new)

    @pl.when(n == pl.num_programs(2) - 1)
    def _():
        sout_ref[0, 0] = s_ref[...]


def gdn_chunks(qkv, p, ba_tile0, a_rows, dt_rows, s0):
    b, l, _ = qkv.shape
    nc = l // GDN_CHUNK

    def row(d, bi, n):
        return n + d * (nc - 1 - 2 * n)

    return pl.pallas_call(
        _gdn_chunk_kernel,
        grid=(2, b, nc),
        in_specs=[
            pl.BlockSpec((1, GDN_CHUNK, GDN_W), lambda d, bi, n: (bi, row(d, bi, n), 0)),
            pl.BlockSpec((1, GDN_CHUNK, GDN_W), lambda d, bi, n: (bi, row(d, bi, n), 1)),
            pl.BlockSpec((1, GDN_CHUNK, GDN_W), lambda d, bi, n: (bi, row(d, bi, n), 2)),
            pl.BlockSpec((1, GDN_CHUNK, LANES), lambda d, bi, n: (bi, row(d, bi, n), ba_tile0 + d)),
            pl.BlockSpec((1, 1, LANES), lambda d, bi, n: (d, 0, 0)),
            pl.BlockSpec((1, 1, LANES), lambda d, bi, n: (d, 0, 0)),
            pl.BlockSpec((1, 1, GDN_HEADS, GDN_DK, GDN_DV), lambda d, bi, n: (d, bi, 0, 0, 0)),
        ],
        out_specs=[
            pl.BlockSpec((1, 1, GDN_CHUNK, GDN_W), lambda d, bi, n: (d, bi, row(d, bi, n), 0)),
            pl.BlockSpec((1, 1, GDN_HEADS, GDN_DK, GDN_DV), lambda d, bi, n: (d, bi, 0, 0, 0)),
        ],
        out_shape=[
            jax.ShapeDtypeStruct((2, b, l, GDN_W), jnp.float32),
            jax.ShapeDtypeStruct((2, b, GDN_HEADS, GDN_DK, GDN_DV), jnp.float32),
        ],
        scratch_shapes=[pltpu.VMEM((GDN_HEADS, GDN_DK, GDN_DV), jnp.float32)],
        compiler_params=_cparams(("parallel", "parallel", "arbitrary"), 32),
        name="gdn_chunks",
    )(qkv, qkv, qkv, p, a_rows, dt_rows, s0)


def _gated_out_kernel(of_ref, ob_ref, z_ref, w_ref, y_ref):
    for h in range(GDN_HEADS):
        sl = slice(h * GDN_DV, (h + 1) * GDN_DV)
        o = of_ref[0, 0, :, sl] + ob_ref[0, 0, :, sl]
        y = o * lax.rsqrt(jnp.mean(o * o, axis=-1, keepdims=True) + EPS)
        y_ref[0, :, sl] = (y * w_ref[...] * _silu(z_ref[0, :, sl])).astype(y_ref.dtype)


def gated_out(o, p, z_tile, w_norm, *, tm):
    _, b, l, _ = o.shape
    tm = min(tm, l)
    return pl.pallas_call(
        _gated_out_kernel,
        grid=(b, l // tm),
        in_specs=[
            pl.BlockSpec((1, 1, tm, GDN_W), lambda bi, i: (0, bi, i, 0)),
            pl.BlockSpec((1, 1, tm, GDN_W), lambda bi, i: (1, bi, i, 0)),
            pl.BlockSpec((1, tm, GDN_W), lambda bi, i: (bi, i, z_tile)),
            pl.BlockSpec((1, GDN_DV), lambda bi, i: (0, 0)),
        ],
        out_specs=pl.BlockSpec((1, tm, GDN_W), lambda bi, i: (bi, i, 0)),
        out_shape=jax.ShapeDtypeStruct((b, l, GDN_W), MXU_DTYPE),
        compiler_params=_cparams(("parallel", "parallel"), 40),
        name="gated_out",
    )(o, o, p, w_norm.reshape(1, GDN_DV))


def _fourier_kernel(x_ref, bd_ref, cs_ref, o_ref, z_ref):
    i = pl.program_id(1)
    l = x_ref.shape[1]

    @pl.when(i == 0)
    def _():
        z = _mm(x_ref[0], bd_ref[...])
        z_ref[0:l, :] = z[:, :FNET_W].astype(z_ref.dtype)
        z_ref[l:2 * l, :] = z[:, FNET_W:].astype(z_ref.dtype)

    o_ref[0] = jnp.dot(cs_ref[...], z_ref[...],
                       preferred_element_type=jnp.float32).astype(o_ref.dtype)


def _dft_tables(l):
    def cs(n, scale):
        k = jnp.arange(n, dtype=jnp.int32)
        ang = ((k[:, None] * k[None, :]) % n).astype(jnp.float32) * (2.0 * math.pi / n)
        return jnp.cos(ang) * scale, jnp.sin(ang) * scale

    c_l, s_l = cs(l, 1.0)
    c_g, s_g = cs(FNET_GW, 1.0 / math.sqrt(l * FNET_GW))
    eye = jnp.eye(FNET_GROUPS, dtype=jnp.float32)
    bd = jnp.concatenate([jnp.kron(eye, c_g), jnp.kron(eye, s_g)], axis=1)
    cs_l = jnp.concatenate([c_l, -s_l], axis=1)
    return bd.astype(MXU_DTYPE), cs_l.astype(MXU_DTYPE)


def fourier_mix(p, f_tile, *, tm):
    b, l, _ = p.shape
    tm = min(tm, l)
    bd, cs_l = _dft_tables(l)
    return pl.pallas_call(
        _fourier_kernel,
        grid=(b, l // tm),
        in_specs=[
            pl.BlockSpec((1, l, FNET_W), lambda bi, i: (bi, 0, f_tile)),
            pl.BlockSpec((FNET_W, 2 * FNET_W), lambda bi, i: (0, 0)),
            pl.BlockSpec((tm, 2 * l), lambda bi, i: (i, 0)),
        ],
        out_specs=pl.BlockSpec((1, tm, FNET_W), lambda bi, i: (bi, i, 0)),
        out_shape=jax.ShapeDtypeStruct((b, l, FNET_W), MXU_DTYPE),
        scratch_shapes=[pltpu.VMEM((2 * l, FNET_W), MXU_DTYPE)],
        compiler_params=_cparams(("parallel", "arbitrary"), 40),
        name="fourier_mix",
    )(p, bd, cs_l)


def _swa_kernel(*refs, use_window):
    if use_window:
        sink_ref, q_ref, k_ref, v_ref, kc_ref, vc_ref, o_ref = refs
    else:
        sink_ref, q_ref, kc_ref, vc_ref, o_ref = refs
    i = pl.program_id(1)
    tq = q_ref.shape[1]
    scale = SWA_DH ** -0.5
    if use_window:
        l = k_ref.shape[1]
        span = SWA_BLOCK + 2 * SWA_WINDOW
        start = i * tq
        base = pl.multiple_of(jnp.clip(start - SWA_WINDOW, 0, l - span), SWA_BLOCK)
        qpos = start + lax.broadcasted_iota(jnp.int32, (tq, span), 0)
        kpos = base + lax.broadcasted_iota(jnp.int32, (tq, span), 1)
        band = jnp.abs(qpos - kpos) <= SWA_WINDOW
    for g in range(SWA_KV):
        ksl = slice(g * SWA_DH, (g + 1) * SWA_DH)
        qg = jnp.concatenate(
            [q_ref[0, :, (g * SWA_GROUP + hh) * SWA_DH:(g * SWA_GROUP + hh + 1) * SWA_DH]
             for hh in range(SWA_GROUP)], axis=0)
        kc = kc_ref[0, :, ksl]
        vc = vc_ref[0, :, ksl]
        lc_all = _mm_nt(qg, kc) * scale
        if use_window:
            kw = k_ref[0, pl.ds(base, span), ksl]
            vw = v_ref[0, pl.ds(base, span), ksl]
            lw_all = _mm_nt(qg, kw) * scale
        outs = []
        for hh in range(SWA_GROUP):
            rs = slice(hh * tq, (hh + 1) * tq)
            sink = sink_ref[g * SWA_GROUP + hh]
            lc = lc_all[rs]
            m = jnp.maximum(jnp.max(lc, axis=-1, keepdims=True), sink)
            if use_window:
                lw = jnp.where(band, lw_all[rs], -jnp.inf)
                m = jnp.maximum(m, jnp.max(lw, axis=-1, keepdims=True))
                pw = jnp.exp(lw - m)
            pc = jnp.exp(lc - m)
            den = jnp.sum(pc, axis=-1, keepdims=True) + jnp.exp(sink - m)
            acc = _mm(pc, vc)
            if use_window:
                den = den + jnp.sum(pw, axis=-1, keepdims=True)
                acc = acc + _mm(pw, vw)
            outs.append(acc / den)
        for pair in range(SWA_GROUP // 2):
            lo = (g * SWA_GROUP + 2 * pair) * SWA_DH
            o_ref[0, :, lo:lo + 2 * SWA_DH] = jnp.concatenate(
                [outs[2 * pair], outs[2 * pair + 1]], axis=1).astype(o_ref.dtype)


def swa_attention(qkv, ckv, sinks, *, use_window):
    b, l, _ = qkv.shape
    c = ckv.shape[1]
    tq = SWA_BLOCK
    kt = SWA_Q_W // SWA_KV_W
    in_specs = [
        pl.BlockSpec(memory_space=pltpu.SMEM),
        pl.BlockSpec((1, tq, SWA_Q_W), lambda bi, i: (bi, i, 0)),
    ]
    args = [sinks, qkv]
    if use_window:
        in_specs += [
            pl.BlockSpec((1, l, SWA_KV_W), lambda bi, i: (bi, 0, kt)),
            pl.BlockSpec((1, l, SWA_KV_W), lambda bi, i: (bi, 0, kt + 1)),
        ]
        args += [qkv, qkv]
    in_specs += [
        pl.BlockSpec((1, c, SWA_KV_W), lambda bi, i: (bi, 0, 0)),
        pl.BlockSpec((1, c, SWA_KV_W), lambda bi, i: (bi, 0, 1)),
    ]
    args += [ckv, ckv]
    return pl.pallas_call(
        functools.partial(_swa_kernel, use_window=use_window),
        grid=(b, l // tq),
        in_specs=in_specs,
        out_specs=pl.BlockSpec((1, tq, SWA_Q_W), lambda bi, i: (bi, i, 0)),
        out_shape=jax.ShapeDtypeStruct((b, l, SWA_Q_W), MXU_DTYPE),
        compiler_params=_cparams(("parallel", "parallel"), 40),
        name="swa_attention",
    )(*args)


def _rope_tables(l):
    rows = l // GRID_W
    r = jnp.broadcast_to(jnp.arange(rows)[:, None], (rows, GRID_W)).reshape(-1).astype(jnp.float32)
    col = jnp.broadcast_to(jnp.arange(GRID_W)[None, :], (rows, GRID_W)).reshape(-1).astype(jnp.float32)
    n = SWA_DH // 4
    freq = ROPE_BASE ** (-jnp.arange(n, dtype=jnp.float32) / n)
    ang_r = r[:, None] * freq
    ang_c = col[:, None] * freq
    cos = jnp.concatenate([jnp.cos(ang_r)] * 2 + [jnp.cos(ang_c)] * 2, axis=-1)
    sin = jnp.concatenate([-jnp.sin(ang_r), jnp.sin(ang_r), -jnp.sin(ang_c), jnp.sin(ang_c)], axis=-1)
    reps = LANES // SWA_DH
    return jnp.tile(cos, (1, reps)), jnp.tile(sin, (1, reps))


def _top_values(s, count):
    vals = []
    for _ in range(count):
        m = jnp.max(s, axis=0, keepdims=True)
        vals.append(m)
        s = jnp.where(s == m, -jnp.inf, s)
    return vals


def _peer_cells():
    n = PEER_TOPK + 1
    return [(a, b) for a in range(n) for b in range(n) if (a + 1) * (b + 1) <= n]


def _peer_topk_kernel(q_ref, keys_ref, s1_ref, e1_ref, d_ref, e0_ref):
    n = PEER_TOPK + 1
    cells = _peer_cells()
    for h in range(PEER_HEADS):
        q0 = q_ref[:, (2 * h) * PEER_HALF:(2 * h + 1) * PEER_HALF]
        q1 = q_ref[:, (2 * h + 1) * PEER_HALF:(2 * h + 2) * PEER_HALF]
        s0 = lax.dot_general(keys_ref[0], q0, _NT, precision=_HI,
                             preferred_element_type=jnp.float32)
        s1 = lax.dot_general(keys_ref[1], q1, _NT, precision=_HI,
                             preferred_element_type=jnp.float32)
        top0 = _top_values(s0, n)
        top1 = _top_values(s1, n)
        rows = [top0[a] + top1[b] for a, b in cells]
        rows += [jnp.full_like(rows[0], -jnp.inf)] * (-len(rows) % 8)
        best = _top_values(jnp.concatenate(rows, axis=0), n)
        thr = 0.5 * (best[PEER_TOPK - 1] + best[PEER_TOPK])
        z = jnp.ones_like(thr)
        for kk in range(1, PEER_TOPK):
            z = z + jnp.exp(best[kk] - best[0])
        s1_ref[h] = s1
        e1_ref[h] = jnp.exp(s1 - top1[0]) / z
        d_ref[h] = thr - s0
        e0_ref[h] = jnp.exp(s0 - top0[0])


def peer_topk(q, keys, *, tt):
    t = q.shape[0]
    tt = min(tt, t)
    out = jax.ShapeDtypeStruct((PEER_HEADS, PEER_NKEYS, t), jnp.float32)
    ospec = pl.BlockSpec((PEER_HEADS, PEER_NKEYS, tt), lambda i: (0, 0, i))
    return pl.pallas_call(
        _peer_topk_kernel,
        grid=(t // tt,),
        in_specs=[
            pl.BlockSpec((tt, q.shape[1]), lambda i: (i, 0)),
            pl.BlockSpec((2, PEER_NKEYS, PEER_HALF), lambda i: (0, 0, 0)),
        ],
        out_specs=[ospec] * 4,
        out_shape=[out] * 4,
        compiler_params=_cparams(("parallel",), 40),
        name="peer_topk",
    )(q, keys)


def _gelu(x):
    return 0.5 * x * (1.0 + lax.erf(x * (2.0 ** -0.5)))


def _peer_dense_kernel(*refs, final_norm):
    (hm_ref, u_ref, vt_ref, s1_ref, e1_ref, d_ref, e0_ref, x_ref, gate_ref) = refs[:9]
    pos = 9
    if final_norm:
        fn_ref = refs[pos]
        pos += 1
    o_ref, acc_ref, pt_ref = refs[pos:pos + 3]
    e = pl.program_id(1)
    te, tm = pt_ref.shape

    @pl.when(e == 0)
    def _():
        acc_ref[...] = jnp.zeros_like(acc_ref)

    a_t = lax.dot_general(u_ref[...], hm_ref[...], _NT,
                          preferred_element_type=jnp.float32)
    for il in range(te // PEER_NKEYS):
        rs = slice(il * PEER_NKEYS, (il + 1) * PEER_NKEYS)
        for tl in range(tm // LANES):
            cs = slice(tl * LANES, (tl + 1) * LANES)
            g = None
            for h in range(PEER_HEADS):
                drow = d_ref[h, 0, il:il + 1, cs]
                erow = e0_ref[h, 0, il:il + 1, cs]
                term = jnp.where(s1_ref[h, :, cs] >= drow, e1_ref[h, :, cs] * erow, 0.0)
                g = term if g is None else g + term
            pt_ref[rs, cs] = (g * _gelu(a_t[rs, cs])).astype(pt_ref.dtype)
    acc_ref[...] += jnp.dot(vt_ref[...], pt_ref[...], preferred_element_type=jnp.float32)

    @pl.when(e == pl.num_programs(1) - 1)
    def _():
        y = x_ref[...] + gate_ref[0] * acc_ref[...].T
        if final_norm:
            y = y * lax.rsqrt(jnp.mean(y * y, axis=-1, keepdims=True) + EPS) * fn_ref[...]
        o_ref[...] = y


def peer_dense(hm, u, vt, sel, x, gate, rows_per_gate, final_gain=None, *, tm, te):
    t, d = x.shape
    ne = u.shape[0]
    tm = min(tm, t, rows_per_gate)
    assert rows_per_gate % tm == 0 and t % tm == 0 and ne % te == 0 and te % PEER_NKEYS == 0
    ni = te // PEER_NKEYS
    s1, e1, dd, e0 = sel
    dd = dd.reshape(PEER_HEADS, PEER_NKEYS // ni, ni, t)
    e0 = e0.reshape(PEER_HEADS, PEER_NKEYS // ni, ni, t)
    gpb = rows_per_gate // tm
    full = pl.BlockSpec((PEER_HEADS, PEER_NKEYS, tm), lambda i, e: (0, 0, i))
    rowsp = pl.BlockSpec((PEER_HEADS, 1, ni, tm), lambda i, e: (0, e, 0, i))
    in_specs = [
        pl.BlockSpec((tm, d), lambda i, e: (i, 0)),
        pl.BlockSpec((te, d), lambda i, e: (e, 0)),
        pl.BlockSpec((d, te), lambda i, e: (0, e)),
        full, full, rowsp, rowsp,
        pl.BlockSpec((tm, d), lambda i, e: (i, 0)),
        pl.BlockSpec((1, 1, d), lambda i, e: (i // gpb, 0, 0)),
    ]
    args = [hm, u, vt, s1, e1, dd, e0, x, gate.reshape(gate.shape[0], 1, d)]
    if final_gain is not None:
        in_specs.append(pl.BlockSpec((1, d), lambda i, e: (0, 0)))
        args.append(final_gain.reshape(1, d))
    return pl.pallas_call(
        functools.partial(_peer_dense_kernel, final_norm=final_gain is not None),
        grid=(t // tm, ne // te),
        in_specs=in_specs,
        out_specs=pl.BlockSpec((tm, d), lambda i, e: (i, 0)),
        out_shape=jax.ShapeDtypeStruct((t, d), jnp.float32),
        scratch_shapes=[pltpu.VMEM((d, tm), jnp.float32), pltpu.VMEM((te, tm), MXU_DTYPE)],
        compiler_params=_cparams(("parallel", "arbitrary"), 56),
        name="peer_dense",
    )(*args)


def peer_block(x, gain, shift, scale, gate, w_q, keys, u, vt, final_gain=None):
    b, l, d = x.shape
    q, hm = mod_matmul(x, gain, shift, scale, w_q, tm=512, tn=512, emit_a=True)
    sel = peer_topk(q.reshape(b * l, d), keys, tt=256)
    out = peer_dense(hm.reshape(b * l, d), u, vt, sel, x.reshape(b * l, d), gate, l,
                     final_gain, tm=512, te=512)
    return out.reshape(b, l, d)


def _pack_even_w_in(w_in):
    d = w_in.shape[0]
    o = 4 * GDN_W
    h = GDN_HEADS
    ba = w_in[:, o:o + 4 * h]
    zeros = jnp.zeros((d, LANES - 2 * h), w_in.dtype)
    fwd = jnp.concatenate([ba[:, 0:h], ba[:, 2 * h:3 * h], zeros], axis=1)
    bwd = jnp.concatenate([ba[:, h:2 * h], ba[:, 3 * h:4 * h], zeros], axis=1)
    pad = jnp.zeros((d, 2 * LANES), w_in.dtype)
    packed = jnp.concatenate([w_in[:, :o], w_in[:, o + 4 * h:], fwd, bwd, pad], axis=1)
    return packed.astype(MXU_DTYPE)


def _lane_rows(vals):
    z = jnp.zeros((2, GDN_HEADS), jnp.float32)
    pad = jnp.zeros((2, LANES - 2 * GDN_HEADS), jnp.float32)
    return jnp.concatenate([z, vals.astype(jnp.float32), pad], axis=1).reshape(2, 1, LANES)


def even_layer_mixer(x, h_c, m_lat, m_ctx, gain, w_in, conv_w, a_log, dt_bias, out_norm, w_out,
                     ctx_out):
    b = x.shape[0]
    w_packed = _pack_even_w_in(w_in)
    ba_tile0 = (4 * GDN_W + FNET_W) // LANES
    z_tile = 3
    f_tile = 4 * GDN_W // FNET_W
    a_rows = _lane_rows(jnp.exp(a_log.astype(jnp.float32)))
    dt_rows = _lane_rows(dt_bias)
    w_out_c = w_out.astype(MXU_DTYPE)
    w_y, w_f = w_out_c[:GDN_W], w_out_c[GDN_W:]

    p_ctx = mod_matmul(h_c, gain, m_ctx[0], m_ctx[1], w_packed, tm=512, tn=512)
    p_lat = mod_matmul(x, gain, m_lat[0], m_lat[1], w_packed, tm=512, tn=512)
    zero = jnp.zeros((2, b, GDN_HEADS, GDN_DK, GDN_DV), jnp.float32)
    o_ctx, s_ctx = gdn_chunks(gdn_short_conv(p_ctx, conv_w), p_ctx, ba_tile0, a_rows, dt_rows, zero)
    o_lat, _ = gdn_chunks(gdn_short_conv(p_lat, conv_w), p_lat, ba_tile0, a_rows, dt_rows, s_ctx)

    def finish(p, o, resid, gate):
        y = gated_out(o, p, z_tile, out_norm, tm=512)
        f = fourier_mix(p, f_tile, tm=512)
        return resid_matmul([y, f], [w_y, w_f], resid, gate, tm=1024, tn=512)

    x = finish(p_lat, o_lat, x, m_lat[2])
    if ctx_out:
        h_c = finish(p_ctx, o_ctx, h_c, m_ctx[2])
    return x, h_c


def odd_layer_mixer(x, h_c, m_lat, m_ctx, gain, w_qkv, sinks, w_out, ctx_out):
    l = x.shape[1]
    w_c = w_qkv.astype(MXU_DTYPE)
    w_out_c = w_out.astype(MXU_DTYPE)
    rope = _rope_tables(l)
    tn = 256
    qkv = mod_matmul(x, gain, m_lat[0], m_lat[1], w_c, tm=512, tn=tn,
                     rope=rope, rope_tiles=(SWA_Q_W + SWA_KV_W) // tn)
    ckv = mod_matmul(h_c, gain, m_ctx[0], m_ctx[1], w_c[:, SWA_Q_W:], tm=512, tn=tn)
    sinks = sinks.astype(jnp.float32)
    o = swa_attention(qkv, ckv, sinks, use_window=True)
    x_new = resid_matmul([o], [w_out_c], x, m_lat[2], tm=1024, tn=512)
    if ctx_out:
        q_c = mod_matmul(h_c, gain, m_ctx[0], m_ctx[1], w_c[:, :SWA_Q_W], tm=512, tn=tn)
        o_c = swa_attention(q_c, ckv, sinks, use_window=False)
        h_c = resid_matmul([o_c], [w_out_c], h_c, m_ctx[2], tm=1024, tn=512)
    return x_new, h_c


def kernel(x, c, ctx, c_ctx, ada_w, ada_b, norm_mix, norm_ffn, even_w_in, gdn_conv, gdn_a_log, gdn_dt_bias, gdn_out_norm, even_w_out, odd_w_qkv, odd_sinks, odd_w_out, peer_w_q, peer_keys, peer_u, peer_v, final_norm):
    b, _, d = x.shape
    depth = ada_w.shape[0]
    rows = 16
    cond = jnp.concatenate([c, c_ctx[None, :], jnp.zeros((rows - b - 1, d), c.dtype)], axis=0)
    mods = ada_mod_all(cond, ada_w, ada_b)
    conv_w_all = gdn_conv
    h_c = ctx
    for i in range(depth):
        last = i == depth - 1
        j = i // 2
        m = mods[i].reshape(rows, N_MOD, d)
        m_lat = [m[:b, k] for k in range(N_MOD)]
        m_ctx = [jnp.broadcast_to(m[b, k][None, :], (b, d)) for k in range(N_MOD)]
        if i % 2 == 0:
            x, h_c = even_layer_mixer(x, h_c, m_lat, m_ctx, norm_mix[i], even_w_in[j], conv_w_all[j],
                                      gdn_a_log[j], gdn_dt_bias[j], gdn_out_norm[j], even_w_out[j],
                                      not last)
        else:
            x, h_c = odd_layer_mixer(x, h_c, m_lat, m_ctx, norm_mix[i], odd_w_qkv[j], odd_sinks[j],
                                     odd_w_out[j], not last)
        w_q = peer_w_q[i].astype(MXU_DTYPE)
        u = peer_u[i].astype(MXU_DTYPE)
        vt = peer_v[i].T.astype(MXU_DTYPE)
        keys = peer_keys[i].astype(jnp.float32)
        x = peer_block(x, norm_ffn[i], m_lat[3], m_lat[4], m_lat[5], w_q, keys, u, vt,
                       final_norm if last else None)
        if not last:
            h_c = peer_block(h_c, norm_ffn[i], m_ctx[3], m_ctx[4], m_ctx[5], w_q, keys, u, vt)
    return x
```

```python
import functools
import math

import jax
import jax.numpy as jnp
from jax import lax
from jax.experimental import pallas as pl
from jax.experimental.pallas import tpu as pltpu

N_MOD = 6
EPS = 1e-6
GDN_HEADS = 12
GDN_DK = 128
GDN_DV = 128
GDN_W = GDN_HEADS * GDN_DV
GDN_CONV = 5
GDN_CHUNK = 64
FNET_GROUPS = 4
FNET_GW = 128
FNET_W = FNET_GROUPS * FNET_GW
GRID_W = 64
SWA_HEADS = 32
SWA_KV = 4
SWA_GROUP = SWA_HEADS // SWA_KV
SWA_DH = 64
SWA_WINDOW = 128
SWA_BLOCK = 128
ROPE_BASE = 10000.0
SWA_Q_W = SWA_HEADS * SWA_DH
SWA_KV_W = SWA_KV * SWA_DH
PEER_HEADS = 8
PEER_NKEYS = 128
PEER_HALF = 128
PEER_TOPK = 16

LANES = 128
VMEM_BYTES_V7X = 64 * 1024 * 1024
MXU_DTYPE = jnp.bfloat16

_HI = lax.Precision.HIGHEST
_NT = (((1,), (1,)), ((), ()))
_TN = (((0,), (0,)), ((), ()))


def _cparams(semantics, vmem_mb):
    assert vmem_mb * 1024 * 1024 < VMEM_BYTES_V7X
    return pltpu.CompilerParams(dimension_semantics=semantics,
                                vmem_limit_bytes=vmem_mb * 1024 * 1024)


def _mm(a, b):
    return jnp.dot(a.astype(MXU_DTYPE), b.astype(MXU_DTYPE),
                   preferred_element_type=jnp.float32)


def _mm_nt(a, b):
    return lax.dot_general(a.astype(MXU_DTYPE), b.astype(MXU_DTYPE), _NT,
                           preferred_element_type=jnp.float32)


def _mm_tn(a, b):
    return lax.dot_general(a.astype(MXU_DTYPE), b.astype(MXU_DTYPE), _TN,
                           preferred_element_type=jnp.float32)


def _mm_f32(a, b):
    return jnp.dot(a, b, precision=_HI, preferred_element_type=jnp.float32)


def _split(a):
    hi = a.astype(MXU_DTYPE)
    lo = (a - hi.astype(jnp.float32)).astype(MXU_DTYPE)
    return hi, lo


def _mm_x3(a, b):
    (ah, al), (bh, bl) = a, b
    dot = functools.partial(jnp.dot, preferred_element_type=jnp.float32)
    return dot(ah, bh) + (dot(ah, bl) + dot(al, bh))


def _sigmoid(x):
    return 1.0 / (1.0 + jnp.exp(-x))


def _silu(x):
    return x * _sigmoid(x)


def _softplus(x):
    return jnp.maximum(x, 0.0) + jnp.log1p(jnp.exp(-jnp.abs(x)))


def _ada_kernel(cond_ref, w_ref, b_ref, o_ref):
    a = _silu(cond_ref[...])
    o_ref[0] = _mm(a, w_ref[0]) + b_ref[0]


def ada_mod_all(cond, ada_w, ada_b, *, tn=1024):
    depth, d, n = ada_w.shape
    r = cond.shape[0]
    return pl.pallas_call(
        _ada_kernel,
        grid=(depth, n // tn),
        in_specs=[
            pl.BlockSpec((r, d), lambda l, j: (0, 0)),
            pl.BlockSpec((1, d, tn), lambda l, j: (l, 0, j)),
            pl.BlockSpec((1, 1, tn), lambda l, j: (l, 0, j)),
        ],
        out_specs=pl.BlockSpec((1, r, tn), lambda l, j: (l, 0, j)),
        out_shape=jax.ShapeDtypeStruct((depth, r, n), jnp.float32),
        compiler_params=_cparams(("parallel", "parallel"), 40),
        name="ada_mod",
    )(cond, ada_w, ada_b.reshape(depth, 1, n))


def _rope_swap(y):
    n = y.shape[-1]
    lane = lax.broadcasted_iota(jnp.int32, y.shape, 1)
    up = pltpu.roll(y, n - 16, 1)
    down = pltpu.roll(y, 16, 1)
    return jnp.where(lane % 32 < 16, up, down)


def _mod_matmul_kernel(*refs, rope_tiles, emit_a, tn):
    x_ref, gain_ref, shift_ref, scale_ref, w_ref = refs[:5]
    pos = 5
    if rope_tiles:
        cos_ref, sin_ref = refs[pos:pos + 2]
        pos += 2
    o_ref = refs[pos]
    pos += 1
    if emit_a:
        a_out_ref = refs[pos]
        pos += 1
    a_ref = refs[pos]
    j = pl.program_id(2)

    @pl.when(j == 0)
    def _():
        x = x_ref[0]
        y = x * lax.rsqrt(jnp.mean(x * x, axis=-1, keepdims=True) + EPS)
        y = y * gain_ref[...]
        a = (y * (1.0 + scale_ref[0]) + shift_ref[0]).astype(a_ref.dtype)
        a_ref[...] = a
        if emit_a:
            a_out_ref[0] = a

    acc = jnp.dot(a_ref[...], w_ref[...], preferred_element_type=jnp.float32)
    if rope_tiles:
        @pl.when(j < rope_tiles)
        def _():
            cos = cos_ref[...]
            sin = sin_ref[...]
            for t in range(tn // LANES):
                sl = slice(t * LANES, (t + 1) * LANES)
                y = acc[:, sl]
                o_ref[0, :, sl] = (y * cos + _rope_swap(y) * sin).astype(o_ref.dtype)

        @pl.when(j >= rope_tiles)
        def _():
            o_ref[0] = acc.astype(o_ref.dtype)
    else:
        o_ref[0] = acc.astype(o_ref.dtype)


def mod_matmul(x, gain, shift, scale, w, *, tm, tn, rope=None, rope_tiles=0, emit_a=False):
    b, l, d = x.shape
    n = w.shape[1]
    tm = min(tm, l)
    assert l % tm == 0 and n % tn == 0 and tn % LANES == 0
    in_specs = [
        pl.BlockSpec((1, tm, d), lambda bi, i, j: (bi, i, 0)),
        pl.BlockSpec((1, d), lambda bi, i, j: (0, 0)),
        pl.BlockSpec((1, 1, d), lambda bi, i, j: (bi, 0, 0)),
        pl.BlockSpec((1, 1, d), lambda bi, i, j: (bi, 0, 0)),
        pl.BlockSpec((d, tn), lambda bi, i, j: (0, j)),
    ]
    args = [x, gain.reshape(1, d), shift.reshape(b, 1, d), scale.reshape(b, 1, d), w]
    if rope_tiles:
        in_specs += [pl.BlockSpec((tm, LANES), lambda bi, i, j: (i, 0))] * 2
        args += list(rope)
    out_specs = [pl.BlockSpec((1, tm, tn), lambda bi, i, j: (bi, i, j))]
    out_shape = [jax.ShapeDtypeStruct((b, l, n), jnp.float32)]
    if emit_a:
        out_specs.append(pl.BlockSpec((1, tm, d), lambda bi, i, j: (bi, i, 0)))
        out_shape.append(jax.ShapeDtypeStruct((b, l, d), w.dtype))
    res = pl.pallas_call(
        functools.partial(_mod_matmul_kernel, rope_tiles=rope_tiles, emit_a=emit_a, tn=tn),
        grid=(b, l // tm, n // tn),
        in_specs=in_specs,
        out_specs=out_specs,
        out_shape=out_shape,
        scratch_shapes=[pltpu.VMEM((tm, d), w.dtype)],
        compiler_params=_cparams(("parallel", "parallel", "arbitrary"), 48),
        name="mod_matmul",
    )(*args)
    return res if emit_a else res[0]


def _resid_matmul_kernel(*refs, n_pairs):
    a_refs = refs[:n_pairs]
    w_refs = refs[n_pairs:2 * n_pairs]
    x_ref, gate_ref, o_ref = refs[2 * n_pairs:]
    acc = jnp.dot(a_refs[0][0], w_refs[0][...], preferred_element_type=jnp.float32)
    for a_ref, w_ref in zip(a_refs[1:], w_refs[1:]):
        acc += jnp.dot(a_ref[0], w_ref[...], preferred_element_type=jnp.float32)
    o_ref[0] = x_ref[0] + gate_ref[0] * acc


def resid_matmul(acts, weights, x, gate, *, tm, tn):
    b, l, n = x.shape
    tm = min(tm, l)
    in_specs = []
    for a in acts:
        in_specs.append(pl.BlockSpec((1, tm, a.shape[2]), lambda bi, i, j: (bi, i, 0)))
    for w in weights:
        in_specs.append(pl.BlockSpec((w.shape[0], tn), lambda bi, i, j: (0, j)))
    in_specs += [
        pl.BlockSpec((1, tm, tn), lambda bi, i, j: (bi, i, j)),
        pl.BlockSpec((1, 1, tn), lambda bi, i, j: (bi, 0, j)),
    ]
    return pl.pallas_call(
        functools.partial(_resid_matmul_kernel, n_pairs=len(acts)),
        grid=(b, l // tm, n // tn),
        in_specs=in_specs,
        out_specs=pl.BlockSpec((1, tm, tn), lambda bi, i, j: (bi, i, j)),
        out_shape=jax.ShapeDtypeStruct((b, l, n), jnp.float32),
        compiler_params=_cparams(("parallel", "parallel", "parallel"), 40),
        name="resid_matmul",
    )(*acts, *weights, x, gate.reshape(b, 1, n))


def _gdn_conv_kernel(p_ref, w_ref, o_ref):
    c = pl.program_id(1)
    x = p_ref[0]
    l = x.shape[0]
    row = lax.broadcasted_iota(jnp.int32, x.shape, 0)
    pad = (GDN_CONV - 1) // 2
    y = x * w_ref[0, pad:pad + 1, :]
    for t in range(GDN_CONV):
        s = t - pad
        if s == 0:
            continue
        xs = pltpu.roll(x, (-s) % l, 0)
        ok = (row + s >= 0) & (row + s < l)
        y = y + jnp.where(ok, xs, 0.0) * w_ref[0, t:t + 1, :]
    y = _silu(y)
    inv = lax.rsqrt(jnp.sum(y * y, axis=-1, keepdims=True) + 1e-6)
    fac = jnp.where(c < 2 * GDN_HEADS, inv, 1.0)
    fac = fac * jnp.where(c < GDN_HEADS, GDN_DK ** -0.5, 1.0)
    o_ref[0] = y * fac


def gdn_short_conv(p, conv_w):
    b, l, _ = p.shape
    nt = 3 * GDN_W // LANES
    w = conv_w.reshape(GDN_CONV, nt, LANES).transpose(1, 0, 2)
    return pl.pallas_call(
        _gdn_conv_kernel,
        grid=(b, nt),
        in_specs=[
            pl.BlockSpec((1, l, LANES), lambda bi, c: (bi, 0, c)),
            pl.BlockSpec((1, GDN_CONV, LANES), lambda bi, c: (c, 0, 0)),
        ],
        out_specs=pl.BlockSpec((1, l, LANES), lambda bi, c: (bi, 0, c)),
        out_shape=jax.ShapeDtypeStruct((b, l, 3 * GDN_W), jnp.float32),
        compiler_params=_cparams(("parallel", "parallel"), 32),
        name="gdn_conv",
    )(p, w)


def _unit_lower_inverse(lms):
    c = lms[0].shape[0]
    eye = (lax.broadcasted_iota(jnp.int32, (c, c), 0)
           == lax.broadcasted_iota(jnp.int32, (c, c), 1)).astype(jnp.float32)
    ps = [-lm for lm in lms]
    ts = [eye + p for p in ps]
    psplit = [_split(p) for p in ps]
    for _ in range(int(math.log2(c)) - 1):
        psplit = [_split(_mm_x3(sp, sp)) for sp in psplit]
        ts = [t + _mm_x3(_split(t), sp) for t, sp in zip(ts, psplit)]
    return ts


def _gdn_chunk_kernel(q_ref, k_ref, v_ref, ba_ref, a_ref, dt_ref, s0_ref,
                      o_ref, sout_ref, s_ref):
    d = pl.program_id(0)
    n = pl.program_id(2)
    c = GDN_CHUNK

    @pl.when(n == 0)
    def _():
        s_ref[...] = s0_ref[0, 0]

    ba = ba_ref[0]
    beta_all = _sigmoid(ba)
    g_all = -a_ref[0] * _softplus(ba + dt_ref[0])
    ii = lax.broadcasted_iota(jnp.int32, (c, c), 0)
    jj = lax.broadcasted_iota(jnp.int32, (c, c), 1)
    sgn = 1 - 2 * d
    diff = (ii - jj) * sgn
    incl = diff >= 0
    strict = diff > 0
    gc_all = _mm_f32(incl.astype(jnp.float32), g_all)
    gtot_all = jnp.sum(g_all, axis=0, keepdims=True)
    gc_all_t = gc_all.T
    eg_all = jnp.exp(gc_all)
    er_all = jnp.exp(gtot_all - gc_all)
    el_all = jnp.exp(gtot_all)

    heads = range(GDN_HEADS)
    sls = [slice(h * GDN_DK, (h + 1) * GDN_DK) for h in heads]
    cols = [GDN_HEADS + h for h in heads]
    ks = [k_ref[0, :, sl] for sl in sls]
    kbs = [k * beta_all[:, h:h + 1] for h, k in zip(heads, ks)]
    decs = [jnp.exp(jnp.where(incl, gc_all[:, c0:c0 + 1] - gc_all_t[c0:c0 + 1, :], -jnp.inf))
            for c0 in cols]
    lms = [jnp.where(strict, _mm_nt(kb, k) * dec, 0.0) for kb, k, dec in zip(kbs, ks, decs)]
    tinvs = _unit_lower_inverse(lms)
    us = [_mm(t, v_ref[0, :, sl] * beta_all[:, h:h + 1]) for h, t, sl in zip(heads, tinvs, sls)]
    ws = [_mm(t, kb * eg_all[:, c0:c0 + 1]) for t, kb, c0 in zip(tinvs, kbs, cols)]
    qs = [q_ref[0, :, sl] for sl in sls]
    attns = [jnp.where(incl, _mm_nt(q, k) * dec, 0.0) for q, k, dec in zip(qs, ks, decs)]
    ss = [s_ref[h] for h in heads]
    v_news = [u - _mm(w, s) for u, w, s in zip(us, ws, ss)]
    for h in heads:
        qg = qs[h] * eg_all[:, cols[h]:cols[h] + 1]
        o_ref[0, 0, :, sls[h]] = _mm(qg, ss[h]) + _mm(attns[h], v_news[h])
    for h in heads:
        kd = ks[h] * er_all[:, cols[h]:cols[h] + 1]
        s_ref[h] = ss[h] * el_all[:, cols[h]:cols[h] + 1] + _mm_tn(kd, v_news[h])

    @pl.when(n == pl.num_programs(2) - 1)
    def _():
        sout_ref[0, 0] = s_ref[...]


def gdn_chunks(qkv, p, ba_tile0, a_rows, dt_rows, s0):
    b, l, _ = qkv.shape
    nc = l // GDN_CHUNK

    def row(d, bi, n):
        return n + d * (nc - 1 - 2 * n)

    return pl.pallas_call(
        _gdn_chunk_kernel,
        grid=(2, b, nc),
        in_specs=[
            pl.BlockSpec((1, GDN_CHUNK, GDN_W), lambda d, bi, n: (bi, row(d, bi, n), 0)),
            pl.BlockSpec((1, GDN_CHUNK, GDN_W), lambda d, bi, n: (bi, row(d, bi, n), 1)),
            pl.BlockSpec((1, GDN_CHUNK, GDN_W), lambda d, bi, n: (bi, row(d, bi, n), 2)),
            pl.BlockSpec((1, GDN_CHUNK, LANES), lambda d, bi, n: (bi, row(d, bi, n), ba_tile0 + d)),
            pl.BlockSpec((1, 1, LANES), lambda d, bi, n: (d, 0, 0)),
            pl.BlockSpec((1, 1, LANES), lambda d, bi, n: (d, 0, 0)),
            pl.BlockSpec((1, 1, GDN_HEADS, GDN_DK, GDN_DV), lambda d, bi, n: (d, bi, 0, 0, 0)),
        ],
        out_specs=[
            pl.BlockSpec((1, 1, GDN_CHUNK, GDN_W), lambda d, bi, n: (d, bi, row(d, bi, n), 0)),
            pl.BlockSpec((1, 1, GDN_HEADS, GDN_DK, GDN_DV), lambda d, bi, n: (d, bi, 0, 0, 0)),
        ],
        out_shape=[
            jax.ShapeDtypeStruct((2, b, l, GDN_W), jnp.float32),
            jax.ShapeDtypeStruct((2, b, GDN_HEADS, GDN_DK, GDN_DV), jnp.float32),
        ],
        scratch_shapes=[pltpu.VMEM((GDN_HEADS, GDN_DK, GDN_DV), jnp.float32)],
        compiler_params=_cparams(("parallel", "parallel", "arbitrary"), 32),
        name="gdn_chunks",
    )(qkv, qkv, qkv, p, a_rows, dt_rows, s0)


def _gated_out_kernel(of_ref, ob_ref, z_ref, w_ref, y_ref):
    for h in range(GDN_HEADS):
        sl = slice(h * GDN_DV, (h + 1) * GDN_DV)
        o = of_ref[0, 0, :, sl] + ob_ref[0, 0, :, sl]
        y = o * lax.rsqrt(jnp.mean(o * o, axis=-1, keepdims=True) + EPS)
        y_ref[0, :, sl] = (y * w_ref[...] * _silu(z_ref[0, :, sl])).astype(y_ref.dtype)


def gated_out(o, p, z_tile, w_norm, *, tm):
    _, b, l, _ = o.shape
    tm = min(tm, l)
    return pl.pallas_call(
        _gated_out_kernel,
        grid=(b, l // tm),
        in_specs=[
            pl.BlockSpec((1, 1, tm, GDN_W), lambda bi, i: (0, bi, i, 0)),
            pl.BlockSpec((1, 1, tm, GDN_W), lambda bi, i: (1, bi, i, 0)),
            pl.BlockSpec((1, tm, GDN_W), lambda bi, i: (bi, i, z_tile)),
            pl.BlockSpec((1, GDN_DV), lambda bi, i: (0, 0)),
        ],
        out_specs=pl.BlockSpec((1, tm, GDN_W), lambda bi, i: (bi, i, 0)),
        out_shape=jax.ShapeDtypeStruct((b, l, GDN_W), MXU_DTYPE),
        compiler_params=_cparams(("parallel", "parallel"), 40),
        name="gated_out",
    )(o, o, p, w_norm.reshape(1, GDN_DV))


def _fourier_kernel(x_ref, bd_ref, cs_ref, o_ref, z_ref):
    i = pl.program_id(1)
    l = x_ref.shape[1]

    @pl.when(i == 0)
    def _():
        z = _mm(x_ref[0], bd_ref[...])
        z_ref[0:l, :] = z[:, :FNET_W].astype(z_ref.dtype)
        z_ref[l:2 * l, :] = z[:, FNET_W:].astype(z_ref.dtype)

    o_ref[0] = jnp.dot(cs_ref[...], z_ref[...],
                       preferred_element_type=jnp.float32).astype(o_ref.dtype)


def _dft_tables(l):
    def cs(n, scale):
        k = jnp.arange(n, dtype=jnp.int32)
        ang = ((k[:, None] * k[None, :]) % n).astype(jnp.float32) * (2.0 * math.pi / n)
        return jnp.cos(ang) * scale, jnp.sin(ang) * scale

    c_l, s_l = cs(l, 1.0)
    c_g, s_g = cs(FNET_GW, 1.0 / math.sqrt(l * FNET_GW))
    eye = jnp.eye(FNET_GROUPS, dtype=jnp.float32)
    bd = jnp.concatenate([jnp.kron(eye, c_g), jnp.kron(eye, s_g)], axis=1)
    cs_l = jnp.concatenate([c_l, -s_l], axis=1)
    return bd.astype(MXU_DTYPE), cs_l.astype(MXU_DTYPE)


def fourier_mix(p, f_tile, *, tm):
    b, l, _ = p.shape
    tm = min(tm, l)
    bd, cs_l = _dft_tables(l)
    return pl.pallas_call(
        _fourier_kernel,
        grid=(b, l // tm),
        in_specs=[
            pl.BlockSpec((1, l, FNET_W), lambda bi, i: (bi, 0, f_tile)),
            pl.BlockSpec((FNET_W, 2 * FNET_W), lambda bi, i: (0, 0)),
            pl.BlockSpec((tm, 2 * l), lambda bi, i: (i, 0)),
        ],
        out_specs=pl.BlockSpec((1, tm, FNET_W), lambda bi, i: (bi, i, 0)),
        out_shape=jax.ShapeDtypeStruct((b, l, FNET_W), MXU_DTYPE),
        scratch_shapes=[pltpu.VMEM((2 * l, FNET_W), MXU_DTYPE)],
        compiler_params=_cparams(("parallel", "arbitrary"), 40),
        name="fourier_mix",
    )(p, bd, cs_l)


def _swa_kernel(*refs, use_window):
    if use_window:
        sink_ref, q_ref, k_ref, v_ref, kc_ref, vc_ref, o_ref = refs
    else:
        sink_ref, q_ref, kc_ref, vc_ref, o_ref = refs
    i = pl.program_id(1)
    tq = q_ref.shape[1]
    scale = SWA_DH ** -0.5
    if use_window:
        l = k_ref.shape[1]
        span = SWA_BLOCK + 2 * SWA_WINDOW
        start = i * tq
        base = pl.multiple_of(jnp.clip(start - SWA_WINDOW, 0, l - span), SWA_BLOCK)
        qpos = start + lax.broadcasted_iota(jnp.int32, (tq, span), 0)
        kpos = base + lax.broadcasted_iota(jnp.int32, (tq, span), 1)
        band = jnp.abs(qpos - kpos) <= SWA_WINDOW
    for g in range(SWA_KV):
        ksl = slice(g * SWA_DH, (g + 1) * SWA_DH)
        qg = jnp.concatenate(
            [q_ref[0, :, (g * SWA_GROUP + hh) * SWA_DH:(g * SWA_GROUP + hh + 1) * SWA_DH]
             for hh in range(SWA_GROUP)], axis=0)
        kc = kc_ref[0, :, ksl]
        vc = vc_ref[0, :, ksl]
        lc_all = _mm_nt(qg, kc) * scale
        if use_window:
            kw = k_ref[0, pl.ds(base, span), ksl]
            vw = v_ref[0, pl.ds(base, span), ksl]
            lw_all = _mm_nt(qg, kw) * scale
        outs = []
        for hh in range(SWA_GROUP):
            rs = slice(hh * tq, (hh + 1) * tq)
            sink = sink_ref[g * SWA_GROUP + hh]
            lc = lc_all[rs]
            m = jnp.maximum(jnp.max(lc, axis=-1, keepdims=True), sink)
            if use_window:
                lw = jnp.where(band, lw_all[rs], -jnp.inf)
                m = jnp.maximum(m, jnp.max(lw, axis=-1, keepdims=True))
                pw = jnp.exp(lw - m)
            pc = jnp.exp(lc - m)
            den = jnp.sum(pc, axis=-1, keepdims=True) + jnp.exp(sink - m)
            acc = _mm(pc, vc)
            if use_window:
                den = den + jnp.sum(pw, axis=-1, keepdims=True)
                acc = acc + _mm(pw, vw)
            outs.append(acc / den)
        for pair in range(SWA_GROUP // 2):
            lo = (g * SWA_GROUP + 2 * pair) * SWA_DH
            o_ref[0, :, lo:lo + 2 * SWA_DH] = jnp.concatenate(
                [outs[2 * pair], outs[2 * pair + 1]], axis=1).astype(o_ref.dtype)


def swa_attention(qkv, ckv, sinks, *, use_window):
    b, l, _ = qkv.shape
    c = ckv.shape[1]
    tq = SWA_BLOCK
    kt = SWA_Q_W // SWA_KV_W
    in_specs = [
        pl.BlockSpec(memory_space=pltpu.SMEM),
        pl.BlockSpec((1, tq, SWA_Q_W), lambda bi, i: (bi, i, 0)),
    ]
    args = [sinks, qkv]
    if use_window:
        in_specs += [
            pl.BlockSpec((1, l, SWA_KV_W), lambda bi, i: (bi, 0, kt)),
            pl.BlockSpec((1, l, SWA_KV_W), lambda bi, i: (bi, 0, kt + 1)),
        ]
        args += [qkv, qkv]
    in_specs += [
        pl.BlockSpec((1, c, SWA_KV_W), lambda bi, i: (bi, 0, 0)),
        pl.BlockSpec((1, c, SWA_KV_W), lambda bi, i: (bi, 0, 1)),
    ]
    args += [ckv, ckv]
    return pl.pallas_call(
        functools.partial(_swa_kernel, use_window=use_window),
        grid=(b, l // tq),
        in_specs=in_specs,
        out_specs=pl.BlockSpec((1, tq, SWA_Q_W), lambda bi, i: (bi, i, 0)),
        out_shape=jax.ShapeDtypeStruct((b, l, SWA_Q_W), MXU_DTYPE),
        compiler_params=_cparams(("parallel", "parallel"), 40),
        name="swa_attention",
    )(*args)


def _rope_tables(l):
    rows = l // GRID_W
    r = jnp.broadcast_to(jnp.arange(rows)[:, None], (rows, GRID_W)).reshape(-1).astype(jnp.float32)
    col = jnp.broadcast_to(jnp.arange(GRID_W)[None, :], (rows, GRID_W)).reshape(-1).astype(jnp.float32)
    n = SWA_DH // 4
    freq = ROPE_BASE ** (-jnp.arange(n, dtype=jnp.float32) / n)
    ang_r = r[:, None] * freq
    ang_c = col[:, None] * freq
    cos = jnp.concatenate([jnp.cos(ang_r)] * 2 + [jnp.cos(ang_c)] * 2, axis=-1)
    sin = jnp.concatenate([-jnp.sin(ang_r), jnp.sin(ang_r), -jnp.sin(ang_c), jnp.sin(ang_c)], axis=-1)
    reps = LANES // SWA_DH
    return jnp.tile(cos, (1, reps)), jnp.tile(sin, (1, reps))


def _top_values(s, count):
    vals = []
    for _ in range(count):
        m = jnp.max(s, axis=0, keepdims=True)
        vals.append(m)
        s = jnp.where(s == m, -jnp.inf, s)
    return vals


def _peer_cells():
    n = PEER_TOPK + 1
    return [(a, b) for a in range(n) for b in range(n) if (a + 1) * (b + 1) <= n]


def _peer_topk_kernel(q_ref, keys_ref, s1_ref, e1_ref, d_ref, e0_ref):
    n = PEER_TOPK + 1
    cells = _peer_cells()
    for h in range(PEER_HEADS):
        q0 = q_ref[:, (2 * h) * PEER_HALF:(2 * h + 1) * PEER_HALF]
        q1 = q_ref[:, (2 * h + 1) * PEER_HALF:(2 * h + 2) * PEER_HALF]
        s0 = lax.dot_general(keys_ref[0], q0, _NT, precision=_HI,
                             preferred_element_type=jnp.float32)
        s1 = lax.dot_general(keys_ref[1], q1, _NT, precision=_HI,
                             preferred_element_type=jnp.float32)
        top0 = _top_values(s0, n)
        top1 = _top_values(s1, n)
        rows = [top0[a] + top1[b] for a, b in cells]
        rows += [jnp.full_like(rows[0], -jnp.inf)] * (-len(rows) % 8)
        best = _top_values(jnp.concatenate(rows, axis=0), n)
        thr = 0.5 * (best[PEER_TOPK - 1] + best[PEER_TOPK])
        z = jnp.ones_like(thr)
        for kk in range(1, PEER_TOPK):
            z = z + jnp.exp(best[kk] - best[0])
        s1_ref[h] = s1
        e1_ref[h] = jnp.exp(s1 - top1[0]) / z
        d_ref[h] = thr - s0
        e0_ref[h] = jnp.exp(s0 - top0[0])


def peer_topk(q, keys, *, tt):
    t = q.shape[0]
    tt = min(tt, t)
    out = jax.ShapeDtypeStruct((PEER_HEADS, PEER_NKEYS, t), jnp.float32)
    ospec = pl.BlockSpec((PEER_HEADS, PEER_NKEYS, tt), lambda i: (0, 0, i))
    return pl.pallas_call(
        _peer_topk_kernel,
        grid=(t // tt,),
        in_specs=[
            pl.BlockSpec((tt, q.shape[1]), lambda i: (i, 0)),
            pl.BlockSpec((2, PEER_NKEYS, PEER_HALF), lambda i: (0, 0, 0)),
        ],
        out_specs=[ospec] * 4,
        out_shape=[out] * 4,
        compiler_params=_cparams(("parallel",), 40),
        name="peer_topk",
    )(q, keys)


def _gelu(x):
    return 0.5 * x * (1.0 + lax.erf(x * (2.0 ** -0.5)))


def _peer_dense_kernel(*refs, final_norm):
    (hm_ref, u_ref, vt_ref, s1_ref, e1_ref, d_ref, e0_ref, x_ref, gate_ref) = refs[:9]
    pos = 9
    if final_norm:
        fn_ref = refs[pos]
        pos += 1
    o_ref, acc_ref, pt_ref = refs[pos:pos + 3]
    e = pl.program_id(1)
    te, tm = pt_ref.shape

    @pl.when(e == 0)
    def _():
        acc_ref[...] = jnp.zeros_like(acc_ref)

    a_t = lax.dot_general(u_ref[...], hm_ref[...], _NT,
                          preferred_element_type=jnp.float32)
    for il in range(te // PEER_NKEYS):
        rs = slice(il * PEER_NKEYS, (il + 1) * PEER_NKEYS)
        for tl in range(tm // LANES):
            cs = slice(tl * LANES, (tl + 1) * LANES)
            g = None
            for h in range(PEER_HEADS):
                drow = d_ref[h, 0, il:il + 1, cs]
                erow = e0_ref[h, 0, il:il + 1, cs]
                term = jnp.where(s1_ref[h, :, cs] >= drow, e1_ref[h, :, cs] * erow, 0.0)
                g = term if g is None else g + term
            pt_ref[rs, cs] = (g * _gelu(a_t[rs, cs])).astype(pt_ref.dtype)
    acc_ref[...] += jnp.dot(vt_ref[...], pt_ref[...], preferred_element_type=jnp.float32)

    @pl.when(e == pl.num_programs(1) - 1)
    def _():
        y = x_ref[...] + gate_ref[0] * acc_ref[...].T
        if final_norm:
            y = y * lax.rsqrt(jnp.mean(y * y, axis=-1, keepdims=True) + EPS) * fn_ref[...]
        o_ref[...] = y


def peer_dense(hm, u, vt, sel, x, gate, rows_per_gate, final_gain=None, *, tm, te):
    t, d = x.shape
    ne = u.shape[0]
    tm = min(tm, t, rows_per_gate)
    assert rows_per_gate % tm == 0 and t % tm == 0 and ne % te == 0 and te % PEER_NKEYS == 0
    ni = te // PEER_NKEYS
    s1, e1, dd, e0 = sel
    dd = dd.reshape(PEER_HEADS, PEER_NKEYS // ni, ni, t)
    e0 = e0.reshape(PEER_HEADS, PEER_NKEYS // ni, ni, t)
    gpb = rows_per_gate // tm
    full = pl.BlockSpec((PEER_HEADS, PEER_NKEYS, tm), lambda i, e: (0, 0, i))
    rowsp = pl.BlockSpec((PEER_HEADS, 1, ni, tm), lambda i, e: (0, e, 0, i))
    in_specs = [
        pl.BlockSpec((tm, d), lambda i, e: (i, 0)),
        pl.BlockSpec((te, d), lambda i, e: (e, 0)),
        pl.BlockSpec((d, te), lambda i, e: (0, e)),
        full, full, rowsp, rowsp,
        pl.BlockSpec((tm, d), lambda i, e: (i, 0)),
        pl.BlockSpec((1, 1, d), lambda i, e: (i // gpb, 0, 0)),
    ]
    args = [hm, u, vt, s1, e1, dd, e0, x, gate.reshape(gate.shape[0], 1, d)]
    if final_gain is not None:
        in_specs.append(pl.BlockSpec((1, d), lambda i, e: (0, 0)))
        args.append(final_gain.reshape(1, d))
    return pl.pallas_call(
        functools.partial(_peer_dense_kernel, final_norm=final_gain is not None),
        grid=(t // tm, ne // te),
        in_specs=in_specs,
        out_specs=pl.BlockSpec((tm, d), lambda i, e: (i, 0)),
        out_shape=jax.ShapeDtypeStruct((t, d), jnp.float32),
        scratch_shapes=[pltpu.VMEM((d, tm), jnp.float32), pltpu.VMEM((te, tm), MXU_DTYPE)],
        compiler_params=_cparams(("parallel", "arbitrary"), 56),
        name="peer_dense",
    )(*args)


def peer_block(x, gain, shift, scale, gate, w_q, keys, u, vt, final_gain=None):
    b, l, d = x.shape
    q, hm = mod_matmul(x, gain, shift, scale, w_q, tm=512, tn=512, emit_a=True)
    sel = peer_topk(q.reshape(b * l, d), keys, tt=256)
    out = peer_dense(hm.reshape(b * l, d), u, vt, sel, x.reshape(b * l, d), gate, l,
                     final_gain, tm=512, te=512)
    return out.reshape(b, l, d)


def _pack_even_w_in(w_in):
    d = w_in.shape[0]
    o = 4 * GDN_W
    h = GDN_HEADS
    ba = w_in[:, o:o + 4 * h]
    zeros = jnp.zeros((d, LANES - 2 * h), w_in.dtype)
    fwd = jnp.concatenate([ba[:, 0:h], ba[:, 2 * h:3 * h], zeros], axis=1)
    bwd = jnp.concatenate([ba[:, h:2 * h], ba[:, 3 * h:4 * h], zeros], axis=1)
    pad = jnp.zeros((d, 2 * LANES), w_in.dtype)
    packed = jnp.concatenate([w_in[:, :o], w_in[:, o + 4 * h:], fwd, bwd, pad], axis=1)
    return packed.astype(MXU_DTYPE)


def _lane_rows(vals):
    z = jnp.zeros((2, GDN_HEADS), jnp.float32)
    pad = jnp.zeros((2, LANES - 2 * GDN_HEADS), jnp.float32)
    return jnp.concatenate([z, vals.astype(jnp.float32), pad], axis=1).reshape(2, 1, LANES)


def even_layer_mixer(x, h_c, m_lat, m_ctx, gain, w_in, conv_w, a_log, dt_bias, out_norm, w_out,
                     ctx_out):
    b = x.shape[0]
    w_packed = _pack_even_w_in(w_in)
    ba_tile0 = (4 * GDN_W + FNET_W) // LANES
    z_tile = 3
    f_tile = 4 * GDN_W // FNET_W
    a_rows = _lane_rows(jnp.exp(a_log.astype(jnp.float32)))
    dt_rows = _lane_rows(dt_bias)
    w_out_c = w_out.astype(MXU_DTYPE)
    w_y, w_f = w_out_c[:GDN_W], w_out_c[GDN_W:]

    p_ctx = mod_matmul(h_c, gain, m_ctx[0], m_ctx[1], w_packed, tm=512, tn=512)
    p_lat = mod_matmul(x, gain, m_lat[0], m_lat[1], w_packed, tm=512, tn=512)
    zero = jnp.zeros((2, b, GDN_HEADS, GDN_DK, GDN_DV), jnp.float32)
    o_ctx, s_ctx = gdn_chunks(gdn_short_conv(p_ctx, conv_w), p_ctx, ba_tile0, a_rows, dt_rows, zero)
    o_lat, _ = gdn_chunks(gdn_short_conv(p_lat, conv_w), p_lat, ba_tile0, a_rows, dt_rows, s_ctx)

    def finish(p, o, resid, gate):
        y = gated_out(o, p, z_tile, out_norm, tm=512)
        f = fourier_mix(p, f_tile, tm=512)
        return resid_matmul([y, f], [w_y, w_f], resid, gate, tm=1024, tn=512)

    x = finish(p_lat, o_lat, x, m_lat[2])
    if ctx_out:
        h_c = finish(p_ctx, o_ctx, h_c, m_ctx[2])
    return x, h_c


def odd_layer_mixer(x, h_c, m_lat, m_ctx, gain, w_qkv, sinks, w_out, ctx_out):
    l = x.shape[1]
    w_c = w_qkv.astype(MXU_DTYPE)
    w_out_c = w_out.astype(MXU_DTYPE)
    rope = _rope_tables(l)
    tn = 256
    qkv = mod_matmul(x, gain, m_lat[0], m_lat[1], w_c, tm=512, tn=tn,
                     rope=rope, rope_tiles=(SWA_Q_W + SWA_KV_W) // tn)
    ckv = mod_matmul(h_c, gain, m_ctx[0], m_ctx[1], w_c[:, SWA_Q_W:], tm=512, tn=tn)
    sinks = sinks.astype(jnp.float32)
    o = swa_attention(qkv, ckv, sinks, use_window=True)
    x_new = resid_matmul([o], [w_out_c], x, m_lat[2], tm=1024, tn=512)
    if ctx_out:
        q_c = mod_matmul(h_c, gain, m_ctx[0], m_ctx[1], w_c[:, :SWA_Q_W], tm=512, tn=tn)
        o_c = swa_attention(q_c, ckv, sinks, use_window=False)
        h_c = resid_matmul([o_c], [w_out_c], h_c, m_ctx[2], tm=1024, tn=512)
    return x_new, h_c


def kernel(x, c, ctx, c_ctx, ada_w, ada_b, norm_mix, norm_ffn, even_w_in, gdn_conv, gdn_a_log, gdn_dt_bias, gdn_out_norm, even_w_out, odd_w_qkv, odd_sinks, odd_w_out, peer_w_q, peer_keys, peer_u, peer_v, final_norm):
    b, _, d = x.shape
    depth = ada_w.shape[0]
    rows = 16
    cond = jnp.concatenate([c, c_ctx[None, :], jnp.zeros((rows - b - 1, d), c.dtype)], axis=0)
    mods = ada_mod_all(cond, ada_w, ada_b)
    conv_w_all = gdn_conv
    h_c = ctx
    for i in range(depth):
        last = i == depth - 1
        j = i // 2
        m = mods[i].reshape(rows, N_MOD, d)
        m_lat = [m[:b, k] for k in range(N_MOD)]
        m_ctx = [jnp.broadcast_to(m[b, k][None, :], (b, d)) for k in range(N_MOD)]
        if i % 2 == 0:
            x, h_c = even_layer_mixer(x, h_c, m_lat, m_ctx, norm_mix[i], even_w_in[j], conv_w_all[j],
                                      gdn_a_log[j], gdn_dt_bias[j], gdn_out_norm[j], even_w_out[j],
                                      not last)
        else:
            x, h_c = odd_layer_mixer(x, h_c, m_lat, m_ctx, norm_mix[i], odd_w_qkv[j], odd_sinks[j],
                                     odd_w_out[j], not last)
        w_q = peer_w_q[i].astype(MXU_DTYPE)
        u = peer_u[i].astype(MXU_DTYPE)
        vt = peer_v[i].T.astype(MXU_DTYPE)
        keys = peer_keys[i].astype(jnp.float32)
        x = peer_block(x, norm_ffn[i], m_lat[3], m_lat[4], m_lat[5], w_q, keys, u, vt,
                       final_norm if last else None)
        if not last:
            h_c = peer_block(h_c, norm_ffn[i], m_ctx[3], m_ctx[4], m_ctx[5], w_q, keys, u, vt)
    return x
```

```python
import functools
import math

import jax
import jax.numpy as jnp
from jax import lax
from jax.experimental import pallas as pl
from jax.experimental.pallas import tpu as pltpu

N_MOD = 6
EPS = 1e-6
GDN_HEADS = 12
GDN_DK = 128
GDN_DV = 128
GDN_W = GDN_HEADS * GDN_DV
GDN_CONV = 5
GDN_CHUNK = 64
GDN_PACK = 2
FNET_GROUPS = 4
FNET_GW = 128
FNET_W = FNET_GROUPS * FNET_GW
GRID_W = 64
SWA_HEADS = 32
SWA_KV = 4
SWA_GROUP = SWA_HEADS // SWA_KV
SWA_DH = 64
SWA_WINDOW = 128
SWA_BLOCK = 128
ROPE_BASE = 10000.0
SWA_Q_W = SWA_HEADS * SWA_DH
SWA_KV_W = SWA_KV * SWA_DH
PEER_HEADS = 8
PEER_NKEYS = 128
PEER_HALF = 128
PEER_TOPK = 16

LANES = 128
SUBLANES = 8
VMEM_BYTES_V7X = 64 * 1024 * 1024
MXU_DTYPE = jnp.bfloat16

_HI = lax.Precision.HIGHEST
_NT = (((1,), (1,)), ((), ()))
_TN = (((0,), (0,)), ((), ()))


def _cparams(semantics, vmem_mb):
    assert vmem_mb * 1024 * 1024 < VMEM_BYTES_V7X
    return pltpu.CompilerParams(dimension_semantics=semantics,
                                vmem_limit_bytes=vmem_mb * 1024 * 1024)


def _mm(a, b):
    return jnp.dot(a.astype(MXU_DTYPE), b.astype(MXU_DTYPE),
                   preferred_element_type=jnp.float32)


def _mm_nt(a, b):
    return lax.dot_general(a.astype(MXU_DTYPE), b.astype(MXU_DTYPE), _NT,
                           preferred_element_type=jnp.float32)


def _mm_tn(a, b):
    return lax.dot_general(a.astype(MXU_DTYPE), b.astype(MXU_DTYPE), _TN,
                           preferred_element_type=jnp.float32)


def _mm_f32(a, b):
    return jnp.dot(a, b, precision=_HI, preferred_element_type=jnp.float32)


def _split(a):
    hi = a.astype(MXU_DTYPE)
    lo = (a - hi.astype(jnp.float32)).astype(MXU_DTYPE)
    return hi, lo


def _mm_x3(a, b):
    (ah, al), (bh, bl) = a, b
    dot = functools.partial(jnp.dot, preferred_element_type=jnp.float32)
    m = ah.shape[0]
    both = dot(jnp.concatenate([ah, al], axis=0), bh)
    return both[:m] + (dot(ah, bl) + both[m:])


def _sigmoid(x):
    return 1.0 / (1.0 + jnp.exp(-x))


def _silu(x):
    return x * _sigmoid(x)


def _softplus(x):
    return jnp.maximum(x, 0.0) + jnp.log1p(jnp.exp(-jnp.abs(x)))


def _ada_kernel(cond_ref, w_ref, b_ref, o_ref):
    a = _silu(cond_ref[...])
    o_ref[0] = _mm(a, w_ref[0]) + b_ref[0]


def ada_mod_all(cond, ada_w, ada_b, *, tn=1024):
    depth, d, n = ada_w.shape
    r = cond.shape[0]
    return pl.pallas_call(
        _ada_kernel,
        grid=(depth, n // tn),
        in_specs=[
            pl.BlockSpec((r, d), lambda l, j: (0, 0)),
            pl.BlockSpec((1, d, tn), lambda l, j: (l, 0, j)),
            pl.BlockSpec((1, 1, tn), lambda l, j: (l, 0, j)),
        ],
        out_specs=pl.BlockSpec((1, r, tn), lambda l, j: (l, 0, j)),
        out_shape=jax.ShapeDtypeStruct((depth, r, n), jnp.float32),
        compiler_params=_cparams(("parallel", "parallel"), 40),
        name="ada_mod",
    )(cond, ada_w, ada_b.reshape(depth, 1, n))


def _rope_swap(y):
    n = y.shape[-1]
    lane = lax.broadcasted_iota(jnp.int32, y.shape, 1)
    up = pltpu.roll(y, n - 16, 1)
    down = pltpu.roll(y, 16, 1)
    return jnp.where(lane % 32 < 16, up, down)


def _mod_matmul_kernel(*refs, rope_tiles, emit_a, tn):
    x_ref, gain_ref, shift_ref, scale_ref, w_ref = refs[:5]
    pos = 5
    if rope_tiles:
        cos_ref, sin_ref = refs[pos:pos + 2]
        pos += 2
    o_ref = refs[pos]
    pos += 1
    if emit_a:
        a_out_ref = refs[pos]
        pos += 1
    a_ref = refs[pos]
    j = pl.program_id(2)

    @pl.when(j == 0)
    def _():
        x = x_ref[0]
        y = x * lax.rsqrt(jnp.mean(x * x, axis=-1, keepdims=True) + EPS)
        y = y * gain_ref[...]
        a = y * (1.0 + scale_ref[0]) + shift_ref[0]
        a_ref[...] = a.astype(a_ref.dtype)
        if emit_a:
            a_out_ref[...] = a.T.astype(a_out_ref.dtype)

    acc = jnp.dot(a_ref[...], w_ref[...], preferred_element_type=jnp.float32)
    if rope_tiles:
        @pl.when(j < rope_tiles)
        def _():
            cos = cos_ref[...]
            sin = sin_ref[...]
            for t in range(tn // LANES):
                sl = slice(t * LANES, (t + 1) * LANES)
                y = acc[:, sl]
                o_ref[0, :, sl] = (y * cos + _rope_swap(y) * sin).astype(o_ref.dtype)

        @pl.when(j >= rope_tiles)
        def _():
            o_ref[0] = acc.astype(o_ref.dtype)
    else:
        o_ref[0] = acc.astype(o_ref.dtype)


def mod_matmul(x, gain, shift, scale, w, *, tm, tn, rope=None, rope_tiles=0, emit_a=False):
    b, l, d = x.shape
    n = w.shape[1]
    tm = min(tm, l)
    assert l % tm == 0 and n % tn == 0 and tn % LANES == 0
    in_specs = [
        pl.BlockSpec((1, tm, d), lambda bi, i, j: (bi, i, 0)),
        pl.BlockSpec((1, d), lambda bi, i, j: (0, 0)),
        pl.BlockSpec((1, 1, d), lambda bi, i, j: (bi, 0, 0)),
        pl.BlockSpec((1, 1, d), lambda bi, i, j: (bi, 0, 0)),
        pl.BlockSpec((d, tn), lambda bi, i, j: (0, j)),
    ]
    args = [x, gain.reshape(1, d), shift.reshape(b, 1, d), scale.reshape(b, 1, d), w]
    if rope_tiles:
        in_specs += [pl.BlockSpec((tm, LANES), lambda bi, i, j: (i, 0))] * 2
        args += list(rope)
    out_specs = [pl.BlockSpec((1, tm, tn), lambda bi, i, j: (bi, i, j))]
    out_shape = [jax.ShapeDtypeStruct((b, l, n), jnp.float32)]
    if emit_a:
        nb = l // tm
        out_specs.append(pl.BlockSpec((d, tm), lambda bi, i, j: (0, bi * nb + i)))
        out_shape.append(jax.ShapeDtypeStruct((d, b * l), w.dtype))
    res = pl.pallas_call(
        functools.partial(_mod_matmul_kernel, rope_tiles=rope_tiles, emit_a=emit_a, tn=tn),
        grid=(b, l // tm, n // tn),
        in_specs=in_specs,
        out_specs=out_specs,
        out_shape=out_shape,
        scratch_shapes=[pltpu.VMEM((tm, d), w.dtype)],
        compiler_params=_cparams(("parallel", "parallel", "arbitrary"), 56),
        name="mod_matmul",
    )(*args)
    return res if emit_a else res[0]


def _resid_matmul_kernel(*refs, n_pairs):
    a_refs = refs[:n_pairs]
    w_refs = refs[n_pairs:2 * n_pairs]
    x_ref, gate_ref, o_ref = refs[2 * n_pairs:]
    acc = jnp.dot(a_refs[0][0], w_refs[0][...], preferred_element_type=jnp.float32)
    for a_ref, w_ref in zip(a_refs[1:], w_refs[1:]):
        acc += jnp.dot(a_ref[0], w_ref[...], preferred_element_type=jnp.float32)
    o_ref[0] = x_ref[0] + gate_ref[0] * acc


def resid_matmul(acts, weights, x, gate, *, tm, tn):
    b, l, n = x.shape
    tm = min(tm, l)
    in_specs = []
    for a in acts:
        in_specs.append(pl.BlockSpec((1, tm, a.shape[2]), lambda bi, i, j: (bi, i, 0)))
    for w in weights:
        in_specs.append(pl.BlockSpec((w.shape[0], tn), lambda bi, i, j: (0, j)))
    in_specs += [
        pl.BlockSpec((1, tm, tn), lambda bi, i, j: (bi, i, j)),
        pl.BlockSpec((1, 1, tn), lambda bi, i, j: (bi, 0, j)),
    ]
    return pl.pallas_call(
        functools.partial(_resid_matmul_kernel, n_pairs=len(acts)),
        grid=(b, l // tm, n // tn),
        in_specs=in_specs,
        out_specs=pl.BlockSpec((1, tm, tn), lambda bi, i, j: (bi, i, j)),
        out_shape=jax.ShapeDtypeStruct((b, l, n), jnp.float32),
        compiler_params=_cparams(("parallel", "parallel", "parallel"), 40),
        name="resid_matmul",
    )(*acts, *weights, x, gate.reshape(b, 1, n))


def _gdn_conv_kernel(p_ref, w_ref, o_ref):
    c = pl.program_id(1)
    x = p_ref[0]
    l = x.shape[0]
    row = lax.broadcasted_iota(jnp.int32, x.shape, 0)
    pad = (GDN_CONV - 1) // 2
    y = x * w_ref[0, pad:pad + 1, :]
    for t in range(GDN_CONV):
        s = t - pad
        if s == 0:
            continue
        xs = pltpu.roll(x, (-s) % l, 0)
        ok = (row + s >= 0) & (row + s < l)
        y = y + jnp.where(ok, xs, 0.0) * w_ref[0, t:t + 1, :]
    y = _silu(y)
    inv = lax.rsqrt(jnp.sum(y * y, axis=-1, keepdims=True) + 1e-6)
    fac = jnp.where(c < 2 * GDN_HEADS, inv, 1.0)
    fac = fac * jnp.where(c < GDN_HEADS, GDN_DK ** -0.5, 1.0)
    o_ref[0] = y * fac


def gdn_short_conv(p, conv_w):
    b, l, _ = p.shape
    nt = 3 * GDN_W // LANES
    w = conv_w.reshape(GDN_CONV, nt, LANES).transpose(1, 0, 2)
    return pl.pallas_call(
        _gdn_conv_kernel,
        grid=(b, nt),
        in_specs=[
            pl.BlockSpec((1, l, LANES), lambda bi, c: (bi, 0, c)),
            pl.BlockSpec((1, GDN_CONV, LANES), lambda bi, c: (c, 0, 0)),
        ],
        out_specs=pl.BlockSpec((1, l, LANES), lambda bi, c: (bi, 0, c)),
        out_shape=jax.ShapeDtypeStruct((b, l, 3 * GDN_W), jnp.float32),
        compiler_params=_cparams(("parallel", "parallel"), 32),
        name="gdn_conv",
    )(p, w)


def _lane_blocks(cat, n):
    shift = int(math.log2(cat.shape[1] // n))
    blk = jnp.right_shift(lax.broadcasted_iota(jnp.int32, cat.shape, 1), shift)
    return [jnp.where(blk == b, cat, 0.0) for b in range(n)]


def _block_diag_split(cat, n):
    hi = cat.astype(MXU_DTYPE).astype(jnp.float32)
    lo = cat - hi
    return tuple(jnp.concatenate(_lane_blocks(x, n), axis=0).astype(MXU_DTYPE) for x in (hi, lo))


def _unit_lower_inverse(lms, n):
    c = lms[0].shape[0]
    row = lax.broadcasted_iota(jnp.int32, (c, n * c), 0)
    lane = lax.broadcasted_iota(jnp.int32, (c, n * c), 1)
    eye = (jnp.bitwise_and(lane, c - 1) == row).astype(jnp.float32)
    ps = [-jnp.concatenate(lms[g:g + n], axis=1) for g in range(0, len(lms), n)]
    ts = [eye + p for p in ps]
    pbds = [_block_diag_split(p, n) for p in ps]
    for _ in range(int(math.log2(c)) - 1):
        ps = [_mm_x3(_split(p), pbd) for p, pbd in zip(ps, pbds)]
        pbds = [_block_diag_split(p, n) for p in ps]
        ts = [t + _mm_x3(_split(t), pbd) for t, pbd in zip(ts, pbds)]
    return ts


def _gdn_chunk_kernel(q_ref, k_ref, v_ref, ba_ref, a_ref, dt_ref, s0_ref,
                      o_ref, sout_ref, s_ref):
    d = pl.program_id(0)
    n = pl.program_id(2)
    c = GDN_CHUNK
    nb = q_ref.shape[0]

    @pl.when(n == 0)
    def _():
        for bb in range(nb):
            s_ref[bb * GDN_HEADS:(bb + 1) * GDN_HEADS] = s0_ref[0, bb]

    ii = lax.broadcasted_iota(jnp.int32, (c, c), 0)
    jj = lax.broadcasted_iota(jnp.int32, (c, c), 1)
    sgn = 1 - 2 * d
    diff = (ii - jj) * sgn
    incl = diff >= 0
    strict = diff > 0
    tri = incl.astype(jnp.float32)

    beta_alls, gc_alls, gc_all_ts, eg_alls, er_alls, el_alls = [], [], [], [], [], []
    for bb in range(nb):
        ba = ba_ref[bb]
        g_all = -a_ref[0] * _softplus(ba + dt_ref[0])
        gc_all = _mm_f32(tri, g_all)
        gtot_all = jnp.sum(g_all, axis=0, keepdims=True)
        beta_alls.append(_sigmoid(ba))
        gc_alls.append(gc_all)
        gc_all_ts.append(gc_all.T)
        eg_alls.append(jnp.exp(gc_all))
        er_alls.append(jnp.exp(gtot_all - gc_all))
        el_alls.append(jnp.exp(gtot_all))

    units = [(bb, h) for bb in range(nb) for h in range(GDN_HEADS)]

    def col(tables, bb, lane):
        return tables[bb][:, lane:lane + 1]

    sls = [slice(h * GDN_DK, (h + 1) * GDN_DK) for _, h in units]
    ks = [k_ref[bb, :, sl] for (bb, _), sl in zip(units, sls)]
    kbs = [k * col(beta_alls, bb, h) for (bb, h), k in zip(units, ks)]
    decs = [jnp.exp(jnp.where(incl, col(gc_alls, bb, GDN_HEADS + h)
                              - gc_all_ts[bb][GDN_HEADS + h:GDN_HEADS + h + 1, :], -jnp.inf))
            for bb, h in units]
    lms = [jnp.where(strict, _mm_nt(kb, k) * dec, 0.0) for kb, k, dec in zip(kbs, ks, decs)]
    us, ws = [], []
    tinvs = _unit_lower_inverse(lms, GDN_PACK)
    for g0, tinv in zip(range(0, len(units), GDN_PACK), tinvs):
        grp = range(g0, g0 + GDN_PACK)
        vb = jnp.concatenate([v_ref[units[x][0], :, sls[x]] * col(beta_alls, *units[x])
                              for x in grp], axis=0)
        kg = jnp.concatenate([kbs[x] * col(eg_alls, units[x][0], GDN_HEADS + units[x][1])
                              for x in grp], axis=0)
        for t_x in _lane_blocks(tinv, GDN_PACK):
            us.append(_mm(t_x, vb))
            ws.append(_mm(t_x, kg))
    qs = [q_ref[bb, :, sl] for (bb, _), sl in zip(units, sls)]
    attns = [jnp.where(incl, _mm_nt(q, k) * dec, 0.0) for q, k, dec in zip(qs, ks, decs)]
    ss = [s_ref[bb * GDN_HEADS + h] for bb, h in units]
    v_news = [u - _mm(w, s) for u, w, s in zip(us, ws, ss)]
    for x, (bb, h) in enumerate(units):
        qg = qs[x] * col(eg_alls, bb, GDN_HEADS + h)
        o_ref[0, bb, :, sls[x]] = _mm(qg, ss[x]) + _mm(attns[x], v_news[x])
    for x, (bb, h) in enumerate(units):
        kd = ks[x] * col(er_alls, bb, GDN_HEADS + h)
        s_ref[bb * GDN_HEADS + h] = (ss[x] * col(el_alls, bb, GDN_HEADS + h)
                                     + _mm_tn(kd, v_news[x]))

    @pl.when(n == pl.num_programs(2) - 1)
    def _():
        for bb in range(nb):
            sout_ref[0, bb] = s_ref[bb * GDN_HEADS:(bb + 1) * GDN_HEADS]


def gdn_chunks(qkv, p, ba_tile0, a_rows, dt_rows, s0, *, nb):
    b, l, _ = qkv.shape
    nc = l // GDN_CHUNK
    nb = math.gcd(nb, b)

    def row(d, bi, n):
        return n + d * (nc - 1 - 2 * n)

    return pl.pallas_call(
        _gdn_chunk_kernel,
        grid=(2, b // nb, nc),
        in_specs=[
            pl.BlockSpec((nb, GDN_CHUNK, GDN_W), lambda d, bi, n: (bi, row(d, bi, n), 0)),
            pl.BlockSpec((nb, GDN_CHUNK, GDN_W), lambda d, bi, n: (bi, row(d, bi, n), 1)),
            pl.BlockSpec((nb, GDN_CHUNK, GDN_W), lambda d, bi, n: (bi, row(d, bi, n), 2)),
            pl.BlockSpec((nb, GDN_CHUNK, LANES), lambda d, bi, n: (bi, row(d, bi, n), ba_tile0 + d)),
            pl.BlockSpec((1, 1, LANES), lambda d, bi, n: (d, 0, 0)),
            pl.BlockSpec((1, 1, LANES), lambda d, bi, n: (d, 0, 0)),
            pl.BlockSpec((1, nb, GDN_HEADS, GDN_DK, GDN_DV), lambda d, bi, n: (d, bi, 0, 0, 0)),
        ],
        out_specs=[
            pl.BlockSpec((1, nb, GDN_CHUNK, GDN_W), lambda d, bi, n: (d, bi, row(d, bi, n), 0)),
            pl.BlockSpec((1, nb, GDN_HEADS, GDN_DK, GDN_DV), lambda d, bi, n: (d, bi, 0, 0, 0)),
        ],
        out_shape=[
            jax.ShapeDtypeStruct((2, b, l, GDN_W), jnp.float32),
            jax.ShapeDtypeStruct((2, b, GDN_HEADS, GDN_DK, GDN_DV), jnp.float32),
        ],
        scratch_shapes=[pltpu.VMEM((nb * GDN_HEADS, GDN_DK, GDN_DV), jnp.float32)],
        compiler_params=_cparams(("parallel", "parallel", "arbitrary"), 40),
        name="gdn_chunks",
    )(qkv, qkv, qkv, p, a_rows, dt_rows, s0)


def _gated_out_kernel(of_ref, ob_ref, z_ref, w_ref, y_ref):
    for h in range(GDN_HEADS):
        sl = slice(h * GDN_DV, (h + 1) * GDN_DV)
        o = of_ref[0, 0, :, sl] + ob_ref[0, 0, :, sl]
        y = o * lax.rsqrt(jnp.mean(o * o, axis=-1, keepdims=True) + EPS)
        y_ref[0, :, sl] = (y * w_ref[...] * _silu(z_ref[0, :, sl])).astype(y_ref.dtype)


def gated_out(o, p, z_tile, w_norm, *, tm):
    _, b, l, _ = o.shape
    tm = min(tm, l)
    return pl.pallas_call(
        _gated_out_kernel,
        grid=(b, l // tm),
        in_specs=[
            pl.BlockSpec((1, 1, tm, GDN_W), lambda bi, i: (0, bi, i, 0)),
            pl.BlockSpec((1, 1, tm, GDN_W), lambda bi, i: (1, bi, i, 0)),
            pl.BlockSpec((1, tm, GDN_W), lambda bi, i: (bi, i, z_tile)),
            pl.BlockSpec((1, GDN_DV), lambda bi, i: (0, 0)),
        ],
        out_specs=pl.BlockSpec((1, tm, GDN_W), lambda bi, i: (bi, i, 0)),
        out_shape=jax.ShapeDtypeStruct((b, l, GDN_W), MXU_DTYPE),
        compiler_params=_cparams(("parallel", "parallel"), 40),
        name="gated_out",
    )(o, o, p, w_norm.reshape(1, GDN_DV))


def _fourier_kernel(x_ref, bd_ref, cs_ref, o_ref, z_ref):
    i = pl.program_id(1)
    l = x_ref.shape[1]

    @pl.when(i == 0)
    def _():
        z = _mm(x_ref[0], bd_ref[...])
        z_ref[0:l, :] = z[:, :FNET_W].astype(z_ref.dtype)
        z_ref[l:2 * l, :] = z[:, FNET_W:].astype(z_ref.dtype)

    o_ref[0] = jnp.dot(cs_ref[...], z_ref[...],
                       preferred_element_type=jnp.float32).astype(o_ref.dtype)


def _dft_tables(l):
    def cs(n, scale):
        k = jnp.arange(n, dtype=jnp.int32)
        ang = ((k[:, None] * k[None, :]) % n).astype(jnp.float32) * (2.0 * math.pi / n)
        return jnp.cos(ang) * scale, jnp.sin(ang) * scale

    c_l, s_l = cs(l, 1.0)
    c_g, s_g = cs(FNET_GW, 1.0 / math.sqrt(l * FNET_GW))
    eye = jnp.eye(FNET_GROUPS, dtype=jnp.float32)
    bd = jnp.concatenate([jnp.kron(eye, c_g), jnp.kron(eye, s_g)], axis=1)
    cs_l = jnp.concatenate([c_l, -s_l], axis=1)
    return bd.astype(MXU_DTYPE), cs_l.astype(MXU_DTYPE)


def fourier_mix(p, f_tile, *, tm):
    b, l, _ = p.shape
    tm = min(tm, l)
    bd, cs_l = _dft_tables(l)
    return pl.pallas_call(
        _fourier_kernel,
        grid=(b, l // tm),
        in_specs=[
            pl.BlockSpec((1, l, FNET_W), lambda bi, i: (bi, 0, f_tile)),
            pl.BlockSpec((FNET_W, 2 * FNET_W), lambda bi, i: (0, 0)),
            pl.BlockSpec((tm, 2 * l), lambda bi, i: (i, 0)),
        ],
        out_specs=pl.BlockSpec((1, tm, FNET_W), lambda bi, i: (bi, i, 0)),
        out_shape=jax.ShapeDtypeStruct((b, l, FNET_W), MXU_DTYPE),
        scratch_shapes=[pltpu.VMEM((2 * l, FNET_W), MXU_DTYPE)],
        compiler_params=_cparams(("parallel", "arbitrary"), 40),
        name="fourier_mix",
    )(p, bd, cs_l)


def _swa_kernel(*refs, use_window):
    if use_window:
        sink_ref, q_ref, k_ref, v_ref, kc_ref, vc_ref, o_ref = refs
    else:
        sink_ref, q_ref, kc_ref, vc_ref, o_ref = refs
    i = pl.program_id(1)
    tq = q_ref.shape[1]
    scale = SWA_DH ** -0.5
    if use_window:
        l = k_ref.shape[1]
        span = SWA_BLOCK + 2 * SWA_WINDOW
        start = i * tq
        base = pl.multiple_of(jnp.clip(start - SWA_WINDOW, 0, l - span), SWA_BLOCK)
        qpos = start + lax.broadcasted_iota(jnp.int32, (tq, span), 0)
        kpos = base + lax.broadcasted_iota(jnp.int32, (tq, span), 1)
        band = jnp.abs(qpos - kpos) <= SWA_WINDOW
    for g in range(SWA_KV):
        ksl = slice(g * SWA_DH, (g + 1) * SWA_DH)
        qg = jnp.concatenate(
            [q_ref[0, :, (g * SWA_GROUP + hh) * SWA_DH:(g * SWA_GROUP + hh + 1) * SWA_DH]
             for hh in range(SWA_GROUP)], axis=0)
        kc = kc_ref[0, :, ksl]
        vc = vc_ref[0, :, ksl]
        lc_all = _mm_nt(qg, kc) * scale
        if use_window:
            kw = k_ref[0, pl.ds(base, span), ksl]
            vw = v_ref[0, pl.ds(base, span), ksl]
            lw_all = _mm_nt(qg, kw) * scale
        outs = []
        for hh in range(SWA_GROUP):
            rs = slice(hh * tq, (hh + 1) * tq)
            sink = sink_ref[g * SWA_GROUP + hh]
            lc = lc_all[rs]
            m = jnp.maximum(jnp.max(lc, axis=-1, keepdims=True), sink)
            if use_window:
                lw = jnp.where(band, lw_all[rs], -jnp.inf)
                m = jnp.maximum(m, jnp.max(lw, axis=-1, keepdims=True))
                pw = jnp.exp(lw - m)
            pc = jnp.exp(lc - m)
            den = jnp.sum(pc, axis=-1, keepdims=True) + jnp.exp(sink - m)
            acc = _mm(pc, vc)
            if use_window:
                den = den + jnp.sum(pw, axis=-1, keepdims=True)
                acc = acc + _mm(pw, vw)
            outs.append(acc / den)
        for pair in range(SWA_GROUP // 2):
            lo = (g * SWA_GROUP + 2 * pair) * SWA_DH
            o_ref[0, :, lo:lo + 2 * SWA_DH] = jnp.concatenate(
                [outs[2 * pair], outs[2 * pair + 1]], axis=1).astype(o_ref.dtype)


def swa_attention(qkv, ckv, sinks, *, use_window):
    b, l, _ = qkv.shape
    c = ckv.shape[1]
    tq = SWA_BLOCK
    kt = SWA_Q_W // SWA_KV_W
    in_specs = [
        pl.BlockSpec(memory_space=pltpu.SMEM),
        pl.BlockSpec((1, tq, SWA_Q_W), lambda bi, i: (bi, i, 0)),
    ]
    args = [sinks, qkv]
    if use_window:
        in_specs += [
            pl.BlockSpec((1, l, SWA_KV_W), lambda bi, i: (bi, 0, kt)),
            pl.BlockSpec((1, l, SWA_KV_W), lambda bi, i: (bi, 0, kt + 1)),
        ]
        args += [qkv, qkv]
    in_specs += [
        pl.BlockSpec((1, c, SWA_KV_W), lambda bi, i: (bi, 0, 0)),
        pl.BlockSpec((1, c, SWA_KV_W), lambda bi, i: (bi, 0, 1)),
    ]
    args += [ckv, ckv]
    return pl.pallas_call(
        functools.partial(_swa_kernel, use_window=use_window),
        grid=(b, l // tq),
        in_specs=in_specs,
        out_specs=pl.BlockSpec((1, tq, SWA_Q_W), lambda bi, i: (bi, i, 0)),
        out_shape=jax.ShapeDtypeStruct((b, l, SWA_Q_W), MXU_DTYPE),
        compiler_params=_cparams(("parallel", "parallel"), 40),
        name="swa_attention",
    )(*args)


def _rope_tables(l):
    rows = l // GRID_W
    r = jnp.broadcast_to(jnp.arange(rows)[:, None], (rows, GRID_W)).reshape(-1).astype(jnp.float32)
    col = jnp.broadcast_to(jnp.arange(GRID_W)[None, :], (rows, GRID_W)).reshape(-1).astype(jnp.float32)
    n = SWA_DH // 4
    freq = ROPE_BASE ** (-jnp.arange(n, dtype=jnp.float32) / n)
    ang_r = r[:, None] * freq
    ang_c = col[:, None] * freq
    cos = jnp.concatenate([jnp.cos(ang_r)] * 2 + [jnp.cos(ang_c)] * 2, axis=-1)
    sin = jnp.concatenate([-jnp.sin(ang_r), jnp.sin(ang_r), -jnp.sin(ang_c), jnp.sin(ang_c)], axis=-1)
    reps = LANES // SWA_DH
    return jnp.tile(cos, (1, reps)), jnp.tile(sin, (1, reps))


def _top_values(s, count):
    vals = []
    for _ in range(count):
        m = jnp.max(s, axis=0, keepdims=True)
        vals.append(m)
        s = jnp.where(s == m, -jnp.inf, s)
    return vals


def _peer_cells():
    n = PEER_TOPK + 1
    return [(a, b) for a in range(n) for b in range(n) if (a + 1) * (b + 1) <= n]


def _peer_topk_kernel(q_ref, keys_ref, s1_ref, e1_ref, d_ref, e0_ref):
    n = PEER_TOPK + 1
    cells = _peer_cells()
    for h in range(PEER_HEADS):
        q0 = q_ref[:, (2 * h) * PEER_HALF:(2 * h + 1) * PEER_HALF]
        q1 = q_ref[:, (2 * h + 1) * PEER_HALF:(2 * h + 2) * PEER_HALF]
        s0 = lax.dot_general(keys_ref[0], q0, _NT, precision=_HI,
                             preferred_element_type=jnp.float32)
        s1 = lax.dot_general(keys_ref[1], q1, _NT, precision=_HI,
                             preferred_element_type=jnp.float32)
        top0 = _top_values(s0, n)
        top1 = _top_values(s1, n)
        rows = [top0[a] + top1[b] for a, b in cells]
        rows += [jnp.full_like(rows[0], -jnp.inf)] * (-len(rows) % 8)
        best = _top_values(jnp.concatenate(rows, axis=0), n)
        thr = 0.5 * (best[PEER_TOPK - 1] + best[PEER_TOPK])
        z = jnp.ones_like(thr)
        for kk in range(1, PEER_TOPK):
            z = z + jnp.exp(best[kk] - best[0])
        s1_ref[h] = s1
        e1_ref[h] = jnp.exp(s1 - top1[0]) / z
        d_ref[h] = thr - s0
        e0_ref[h] = jnp.exp(s0 - top0[0])


def peer_topk(q, keys, *, tt):
    t = q.shape[0]
    tt = min(tt, t)
    out = jax.ShapeDtypeStruct((PEER_HEADS, PEER_NKEYS, t), jnp.float32)
    ospec = pl.BlockSpec((PEER_HEADS, PEER_NKEYS, tt), lambda i: (0, 0, i))
    return pl.pallas_call(
        _peer_topk_kernel,
        grid=(t // tt,),
        in_specs=[
            pl.BlockSpec((tt, q.shape[1]), lambda i: (i, 0)),
            pl.BlockSpec((2, PEER_NKEYS, PEER_HALF), lambda i: (0, 0, 0)),
        ],
        out_specs=[ospec] * 4,
        out_shape=[out] * 4,
        compiler_params=_cparams(("parallel",), 40),
        name="peer_topk",
    )(q, keys)


def _gelu(x):
    return 0.5 * x * (1.0 + lax.erf(x * (2.0 ** -0.5)))


def _peer_dense_kernel(*refs, final_norm):
    (hmt_ref, u0_ref, ua_ref, ub_ref, vt_ref, s1_ref, e1_ref, d_ref, e0_ref, x_ref,
     gate_ref) = refs[:11]
    pos = 11
    if final_norm:
        fn_ref = refs[pos]
        pos += 1
    o_ref, acc_ref, pt_ref, at_ref = refs[pos:pos + 4]
    e = pl.program_id(1)
    eg, tm = at_ref.shape
    ni = eg // PEER_NKEYS

    def first_matmul(u_ref):
        return jnp.dot(u_ref[...], hmt_ref[...], preferred_element_type=jnp.float32)

    @pl.when(e == 0)
    def _():
        acc_ref[...] = jnp.zeros_like(acc_ref)
        at_ref[...] = first_matmul(u0_ref)

    def gate_group(r, a_block):
        for il in range(ni):
            ig = r * ni + il
            rs = slice(ig * PEER_NKEYS, (ig + 1) * PEER_NKEYS)
            for tl in range(tm // LANES):
                cs = slice(tl * LANES, (tl + 1) * LANES)
                g = None
                for h in range(PEER_HEADS):
                    drow = d_ref[h, ig:ig + 1, cs]
                    erow = e0_ref[h, ig:ig + 1, cs]
                    term = jnp.where(s1_ref[h, :, cs] >= drow, e1_ref[h, :, cs] * erow, 0.0)
                    g = term if g is None else g + term
                pt_ref[rs, cs] = (g * _gelu(a_block(il, cs))).astype(pt_ref.dtype)

    def second_matmul(r):
        es = slice(r * eg, (r + 1) * eg)
        acc_ref[...] += jnp.dot(vt_ref[:, es], pt_ref[es, :], preferred_element_type=jnp.float32)

    a1 = first_matmul(ua_ref)
    gate_group(0, lambda il, cs: at_ref[il * PEER_NKEYS:(il + 1) * PEER_NKEYS, cs])
    second_matmul(0)
    at_ref[...] = first_matmul(ub_ref)
    gate_group(1, lambda il, cs: a1[il * PEER_NKEYS:(il + 1) * PEER_NKEYS, cs])
    second_matmul(1)

    @pl.when(e == pl.num_programs(1) - 1)
    def _():
        y = x_ref[...] + gate_ref[0] * acc_ref[...].T
        if final_norm:
            y = y * lax.rsqrt(jnp.mean(y * y, axis=-1, keepdims=True) + EPS) * fn_ref[...]
        o_ref[...] = y


def peer_dense(hmt, u, vt, sel, x, gate, rows_per_gate, final_gain=None, *, tm, eg):
    t, d = x.shape
    ne = u.shape[0]
    tm = min(tm, t, rows_per_gate)
    te = 2 * eg
    assert rows_per_gate % tm == 0 and t % tm == 0 and ne % te == 0 and eg % PEER_NKEYS == 0
    ni = te // PEER_NKEYS
    assert ni % SUBLANES == 0
    s1, e1, dd, e0 = sel
    gpb = rows_per_gate // tm
    last = ne // eg - 1
    once = pl.Buffered(1)
    full = pl.BlockSpec((PEER_HEADS, PEER_NKEYS, tm), lambda i, e: (0, 0, i))
    rowsp = pl.BlockSpec((PEER_HEADS, ni, tm), lambda i, e: (0, e, i))
    in_specs = [
        pl.BlockSpec((d, tm), lambda i, e: (0, i)),
        pl.BlockSpec((eg, d), lambda i, e: (0, 0), pipeline_mode=once),
        pl.BlockSpec((eg, d), lambda i, e: (2 * e + 1, 0)),
        pl.BlockSpec((eg, d), lambda i, e: (jnp.minimum(2 * e + 2, last), 0)),
        pl.BlockSpec((d, te), lambda i, e: (0, e)),
        full, full, rowsp, rowsp,
        pl.BlockSpec((tm, d), lambda i, e: (i, 0), pipeline_mode=once),
        pl.BlockSpec((1, 1, d), lambda i, e: (i // gpb, 0, 0)),
    ]
    args = [hmt, u, u, u, vt, s1, e1, dd, e0, x, gate.reshape(gate.shape[0], 1, d)]
    if final_gain is not None:
        in_specs.append(pl.BlockSpec((1, d), lambda i, e: (0, 0)))
        args.append(final_gain.reshape(1, d))
    return pl.pallas_call(
        functools.partial(_peer_dense_kernel, final_norm=final_gain is not None),
        grid=(t // tm, ne // te),
        in_specs=in_specs,
        out_specs=pl.BlockSpec((tm, d), lambda i, e: (i, 0)),
        out_shape=jax.ShapeDtypeStruct((t, d), jnp.float32),
        scratch_shapes=[pltpu.VMEM((d, tm), jnp.float32), pltpu.VMEM((te, tm), MXU_DTYPE),
                        pltpu.VMEM((eg, tm), jnp.float32)],
        compiler_params=_cparams(("arbitrary", "arbitrary"), 60),
        name="peer_dense",
    )(*args)


def peer_block(x, gain, shift, scale, gate, w_q, keys, u, vt, final_gain=None):
    b, l, d = x.shape
    q, hmt = mod_matmul(x, gain, shift, scale, w_q, tm=1024, tn=512, emit_a=True)
    sel = peer_topk(q.reshape(b * l, d), keys, tt=256)
    out = peer_dense(hmt, u, vt, sel, x.reshape(b * l, d), gate, l,
                     final_gain, tm=512, eg=512)
    return out.reshape(b, l, d)


def _pack_even_w_in(w_in):
    d = w_in.shape[0]
    o = 4 * GDN_W
    h = GDN_HEADS
    ba = w_in[:, o:o + 4 * h]
    zeros = jnp.zeros((d, LANES - 2 * h), w_in.dtype)
    fwd = jnp.concatenate([ba[:, 0:h], ba[:, 2 * h:3 * h], zeros], axis=1)
    bwd = jnp.concatenate([ba[:, h:2 * h], ba[:, 3 * h:4 * h], zeros], axis=1)
    pad = jnp.zeros((d, 2 * LANES), w_in.dtype)
    packed = jnp.concatenate([w_in[:, :o], w_in[:, o + 4 * h:], fwd, bwd, pad], axis=1)
    return packed.astype(MXU_DTYPE)


def _lane_rows(vals):
    z = jnp.zeros((2, GDN_HEADS), jnp.float32)
    pad = jnp.zeros((2, LANES - 2 * GDN_HEADS), jnp.float32)
    return jnp.concatenate([z, vals.astype(jnp.float32), pad], axis=1).reshape(2, 1, LANES)


def even_layer_mixer(x, h_c, m_lat, m_ctx, gain, w_in, conv_w, a_log, dt_bias, out_norm, w_out,
                     ctx_out):
    b = x.shape[0]
    w_packed = _pack_even_w_in(w_in)
    ba_tile0 = (4 * GDN_W + FNET_W) // LANES
    z_tile = 3
    f_tile = 4 * GDN_W // FNET_W
    a_rows = _lane_rows(jnp.exp(a_log.astype(jnp.float32)))
    dt_rows = _lane_rows(dt_bias)
    w_out_c = w_out.astype(MXU_DTYPE)
    w_y, w_f = w_out_c[:GDN_W], w_out_c[GDN_W:]

    p_ctx = mod_matmul(h_c, gain, m_ctx[0], m_ctx[1], w_packed, tm=512, tn=512)
    p_lat = mod_matmul(x, gain, m_lat[0], m_lat[1], w_packed, tm=1024, tn=512)
    zero = jnp.zeros((2, b, GDN_HEADS, GDN_DK, GDN_DV), jnp.float32)
    o_ctx, s_ctx = gdn_chunks(gdn_short_conv(p_ctx, conv_w), p_ctx, ba_tile0, a_rows, dt_rows, zero,
                              nb=2)
    o_lat, _ = gdn_chunks(gdn_short_conv(p_lat, conv_w), p_lat, ba_tile0, a_rows, dt_rows, s_ctx,
                          nb=2)

    def finish(p, o, resid, gate):
        y = gated_out(o, p, z_tile, out_norm, tm=512)
        f = fourier_mix(p, f_tile, tm=512)
        return resid_matmul([y, f], [w_y, w_f], resid, gate, tm=1024, tn=512)

    x = finish(p_lat, o_lat, x, m_lat[2])
    if ctx_out:
        h_c = finish(p_ctx, o_ctx, h_c, m_ctx[2])
    return x, h_c


def odd_layer_mixer(x, h_c, m_lat, m_ctx, gain, w_qkv, sinks, w_out, ctx_out):
    l = x.shape[1]
    w_c = w_qkv.astype(MXU_DTYPE)
    w_out_c = w_out.astype(MXU_DTYPE)
    rope = _rope_tables(l)
    tn = 256
    qkv = mod_matmul(x, gain, m_lat[0], m_lat[1], w_c, tm=1024, tn=tn,
                     rope=rope, rope_tiles=(SWA_Q_W + SWA_KV_W) // tn)
    ckv = mod_matmul(h_c, gain, m_ctx[0], m_ctx[1], w_c[:, SWA_Q_W:], tm=512, tn=tn)
    sinks = sinks.astype(jnp.float32)
    o = swa_attention(qkv, ckv, sinks, use_window=True)
    x_new = resid_matmul([o], [w_out_c], x, m_lat[2], tm=1024, tn=512)
    if ctx_out:
        q_c = mod_matmul(h_c, gain, m_ctx[0], m_ctx[1], w_c[:, :SWA_Q_W], tm=512, tn=tn)
        o_c = swa_attention(q_c, ckv, sinks, use_window=False)
        h_c = resid_matmul([o_c], [w_out_c], h_c, m_ctx[2], tm=1024, tn=512)
    return x_new, h_c


def kernel(x, c, ctx, c_ctx, ada_w, ada_b, norm_mix, norm_ffn, even_w_in, gdn_conv, gdn_a_log, gdn_dt_bias, gdn_out_norm, even_w_out, odd_w_qkv, odd_sinks, odd_w_out, peer_w_q, peer_keys, peer_u, peer_v, final_norm):
    b, _, d = x.shape
    depth = ada_w.shape[0]
    rows = 16
    cond = jnp.concatenate([c, c_ctx[None, :], jnp.zeros((rows - b - 1, d), c.dtype)], axis=0)
    mods = ada_mod_all(cond, ada_w, ada_b)
    conv_w_all = gdn_conv
    h_c = ctx
    for i in range(depth):
        last = i == depth - 1
        j = i // 2
        m = mods[i].reshape(rows, N_MOD, d)
        m_lat = [m[:b, k] for k in range(N_MOD)]
        m_ctx = [jnp.broadcast_to(m[b, k][None, :], (b, d)) for k in range(N_MOD)]
        if i % 2 == 0:
            x, h_c = even_layer_mixer(x, h_c, m_lat, m_ctx, norm_mix[i], even_w_in[j], conv_w_all[j],
                                      gdn_a_log[j], gdn_dt_bias[j], gdn_out_norm[j], even_w_out[j],
                                      not last)
        else:
            x, h_c = odd_layer_mixer(x, h_c, m_lat, m_ctx, norm_mix[i], odd_w_qkv[j], odd_sinks[j],
                                     odd_w_out[j], not last)
        w_q = peer_w_q[i].astype(MXU_DTYPE)
        u = peer_u[i].astype(MXU_DTYPE)
        vt = peer_v[i].T.astype(MXU_DTYPE)
        keys = peer_keys[i].astype(jnp.float32)
        x = peer_block(x, norm_ffn[i], m_lat[3], m_lat[4], m_lat[5], w_q, keys, u, vt,
                       final_norm if last else None)
        if not last:
            h_c = peer_block(h_c, norm_ffn[i], m_ctx[3], m_ctx[4], m_ctx[5], w_q, keys, u, vt)
    return x
```

```python
import functools
import math

import jax
import jax.numpy as jnp
from jax import lax
from jax.experimental import pallas as pl
from jax.experimental.pallas import tpu as pltpu

N_MOD = 6
EPS = 1e-6
GDN_HEADS = 12
GDN_DK = 128
GDN_DV = 128
GDN_W = GDN_HEADS * GDN_DV
GDN_CONV = 5
GDN_CHUNK = 64
GDN_PACK = 2
FNET_GROUPS = 4
FNET_GW = 128
FNET_W = FNET_GROUPS * FNET_GW
GRID_W = 64
SWA_HEADS = 32
SWA_KV = 4
SWA_GROUP = SWA_HEADS // SWA_KV
SWA_DH = 64
SWA_WINDOW = 128
SWA_BLOCK = 128
ROPE_BASE = 10000.0
SWA_Q_W = SWA_HEADS * SWA_DH
SWA_KV_W = SWA_KV * SWA_DH
PEER_HEADS = 8
PEER_NKEYS = 128
PEER_HALF = 128
PEER_TOPK = 16

LANES = 128
SUBLANES = 8
VMEM_BYTES_V7X = 64 * 1024 * 1024
MXU_DTYPE = jnp.bfloat16

_HI = lax.Precision.HIGHEST
_NT = (((1,), (1,)), ((), ()))
_TN = (((0,), (0,)), ((), ()))


def _cparams(semantics, vmem_mb):
    assert vmem_mb * 1024 * 1024 < VMEM_BYTES_V7X
    return pltpu.CompilerParams(dimension_semantics=semantics,
                                vmem_limit_bytes=vmem_mb * 1024 * 1024)


def _mm(a, b):
    return jnp.dot(a.astype(MXU_DTYPE), b.astype(MXU_DTYPE),
                   preferred_element_type=jnp.float32)


def _mm_nt(a, b):
    return lax.dot_general(a.astype(MXU_DTYPE), b.astype(MXU_DTYPE), _NT,
                           preferred_element_type=jnp.float32)


def _mm_tn(a, b):
    return lax.dot_general(a.astype(MXU_DTYPE), b.astype(MXU_DTYPE), _TN,
                           preferred_element_type=jnp.float32)


def _mm_f32(a, b):
    return jnp.dot(a, b, precision=_HI, preferred_element_type=jnp.float32)


def _split(a):
    hi = a.astype(MXU_DTYPE)
    lo = (a - hi.astype(jnp.float32)).astype(MXU_DTYPE)
    return hi, lo


def _mm_x3(a, b):
    (ah, al), (bh, bl) = a, b
    dot = functools.partial(jnp.dot, preferred_element_type=jnp.float32)
    m = ah.shape[0]
    both = dot(jnp.concatenate([ah, al], axis=0), bh)
    return both[:m] + (dot(ah, bl) + both[m:])


def _sigmoid(x):
    return 1.0 / (1.0 + jnp.exp(-x))


def _silu(x):
    return x * _sigmoid(x)


def _softplus(x):
    return jnp.maximum(x, 0.0) + jnp.log1p(jnp.exp(-jnp.abs(x)))


def _ada_kernel(cond_ref, w_ref, b_ref, o_ref):
    a = _silu(cond_ref[...])
    o_ref[0] = _mm(a, w_ref[0]) + b_ref[0]


def ada_mod_all(cond, ada_w, ada_b, *, tn=1024):
    depth, d, n = ada_w.shape
    r = cond.shape[0]
    return pl.pallas_call(
        _ada_kernel,
        grid=(depth, n // tn),
        in_specs=[
            pl.BlockSpec((r, d), lambda l, j: (0, 0)),
            pl.BlockSpec((1, d, tn), lambda l, j: (l, 0, j)),
            pl.BlockSpec((1, 1, tn), lambda l, j: (l, 0, j)),
        ],
        out_specs=pl.BlockSpec((1, r, tn), lambda l, j: (l, 0, j)),
        out_shape=jax.ShapeDtypeStruct((depth, r, n), jnp.float32),
        compiler_params=_cparams(("parallel", "parallel"), 40),
        name="ada_mod",
    )(cond, ada_w, ada_b.reshape(depth, 1, n))


def _rope_swap(y):
    n = y.shape[-1]
    lane = lax.broadcasted_iota(jnp.int32, y.shape, 1)
    up = pltpu.roll(y, n - 16, 1)
    down = pltpu.roll(y, 16, 1)
    return jnp.where(lane % 32 < 16, up, down)


def _mod_matmul_kernel(*refs, rope_tiles, emit_a, tn):
    x_ref, gain_ref, shift_ref, scale_ref, w_ref = refs[:5]
    pos = 5
    if rope_tiles:
        cos_ref, sin_ref = refs[pos:pos + 2]
        pos += 2
    o_ref = refs[pos]
    pos += 1
    if emit_a:
        a_out_ref = refs[pos]
        pos += 1
    a_ref = refs[pos]
    j = pl.program_id(2)

    @pl.when(j == 0)
    def _():
        x = x_ref[0]
        y = x * lax.rsqrt(jnp.mean(x * x, axis=-1, keepdims=True) + EPS)
        y = y * gain_ref[...]
        a = y * (1.0 + scale_ref[0]) + shift_ref[0]
        a_ref[...] = a.astype(a_ref.dtype)
        if emit_a:
            a_out_ref[...] = a.T.astype(a_out_ref.dtype)

    acc = jnp.dot(a_ref[...], w_ref[...], preferred_element_type=jnp.float32)
    if rope_tiles:
        @pl.when(j < rope_tiles)
        def _():
            cos = cos_ref[...]
            sin = sin_ref[...]
            for t in range(tn // LANES):
                sl = slice(t * LANES, (t + 1) * LANES)
                y = acc[:, sl]
                o_ref[0, :, sl] = (y * cos + _rope_swap(y) * sin).astype(o_ref.dtype)

        @pl.when(j >= rope_tiles)
        def _():
            o_ref[0] = acc.astype(o_ref.dtype)
    else:
        o_ref[0] = acc.astype(o_ref.dtype)


def mod_matmul(x, gain, shift, scale, w, *, tm, tn, rope=None, rope_tiles=0, emit_a=False):
    b, l, d = x.shape
    n = w.shape[1]
    tm = min(tm, l)
    assert l % tm == 0 and n % tn == 0 and tn % LANES == 0
    in_specs = [
        pl.BlockSpec((1, tm, d), lambda bi, i, j: (bi, i, 0)),
        pl.BlockSpec((1, d), lambda bi, i, j: (0, 0)),
        pl.BlockSpec((1, 1, d), lambda bi, i, j: (bi, 0, 0)),
        pl.BlockSpec((1, 1, d), lambda bi, i, j: (bi, 0, 0)),
        pl.BlockSpec((d, tn), lambda bi, i, j: (0, j)),
    ]
    args = [x, gain.reshape(1, d), shift.reshape(b, 1, d), scale.reshape(b, 1, d), w]
    if rope_tiles:
        in_specs += [pl.BlockSpec((tm, LANES), lambda bi, i, j: (i, 0))] * 2
        args += list(rope)
    out_specs = [pl.BlockSpec((1, tm, tn), lambda bi, i, j: (bi, i, j))]
    out_shape = [jax.ShapeDtypeStruct((b, l, n), jnp.float32)]
    if emit_a:
        nb = l // tm
        out_specs.append(pl.BlockSpec((d, tm), lambda bi, i, j: (0, bi * nb + i)))
        out_shape.append(jax.ShapeDtypeStruct((d, b * l), w.dtype))
    res = pl.pallas_call(
        functools.partial(_mod_matmul_kernel, rope_tiles=rope_tiles, emit_a=emit_a, tn=tn),
        grid=(b, l // tm, n // tn),
        in_specs=in_specs,
        out_specs=out_specs,
        out_shape=out_shape,
        scratch_shapes=[pltpu.VMEM((tm, d), w.dtype)],
        compiler_params=_cparams(("parallel", "parallel", "arbitrary"), 56),
        name="mod_matmul",
    )(*args)
    return res if emit_a else res[0]


def _resid_matmul_kernel(*refs, n_pairs):
    a_refs = refs[:n_pairs]
    w_refs = refs[n_pairs:2 * n_pairs]
    x_ref, gate_ref, o_ref = refs[2 * n_pairs:]
    acc = jnp.dot(a_refs[0][0], w_refs[0][...], preferred_element_type=jnp.float32)
    for a_ref, w_ref in zip(a_refs[1:], w_refs[1:]):
        acc += jnp.dot(a_ref[0], w_ref[...], preferred_element_type=jnp.float32)
    o_ref[0] = x_ref[0] + gate_ref[0] * acc


def resid_matmul(acts, weights, x, gate, *, tm, tn):
    b, l, n = x.shape
    tm = min(tm, l)
    in_specs = []
    for a in acts:
        in_specs.append(pl.BlockSpec((1, tm, a.shape[2]), lambda bi, i, j: (bi, i, 0)))
    for w in weights:
        in_specs.append(pl.BlockSpec((w.shape[0], tn), lambda bi, i, j: (0, j)))
    in_specs += [
        pl.BlockSpec((1, tm, tn), lambda bi, i, j: (bi, i, j)),
        pl.BlockSpec((1, 1, tn), lambda bi, i, j: (bi, 0, j)),
    ]
    return pl.pallas_call(
        functools.partial(_resid_matmul_kernel, n_pairs=len(acts)),
        grid=(b, l // tm, n // tn),
        in_specs=in_specs,
        out_specs=pl.BlockSpec((1, tm, tn), lambda bi, i, j: (bi, i, j)),
        out_shape=jax.ShapeDtypeStruct((b, l, n), jnp.float32),
        compiler_params=_cparams(("parallel", "parallel", "parallel"), 40),
        name="resid_matmul",
    )(*acts, *weights, x, gate.reshape(b, 1, n))


def _gdn_conv_kernel(p_ref, w_ref, o_ref):
    c = pl.program_id(1)
    x = p_ref[0]
    l = x.shape[0]
    row = lax.broadcasted_iota(jnp.int32, x.shape, 0)
    pad = (GDN_CONV - 1) // 2
    y = x * w_ref[0, pad:pad + 1, :]
    for t in range(GDN_CONV):
        s = t - pad
        if s == 0:
            continue
        xs = pltpu.roll(x, (-s) % l, 0)
        ok = (row + s >= 0) & (row + s < l)
        y = y + jnp.where(ok, xs, 0.0) * w_ref[0, t:t + 1, :]
    y = _silu(y)
    inv = lax.rsqrt(jnp.sum(y * y, axis=-1, keepdims=True) + 1e-6)
    fac = jnp.where(c < 2 * GDN_HEADS, inv, 1.0)
    fac = fac * jnp.where(c < GDN_HEADS, GDN_DK ** -0.5, 1.0)
    o_ref[0] = y * fac


def gdn_short_conv(p, conv_w):
    b, l, _ = p.shape
    nt = 3 * GDN_W // LANES
    w = conv_w.reshape(GDN_CONV, nt, LANES).transpose(1, 0, 2)
    return pl.pallas_call(
        _gdn_conv_kernel,
        grid=(b, nt),
        in_specs=[
            pl.BlockSpec((1, l, LANES), lambda bi, c: (bi, 0, c)),
            pl.BlockSpec((1, GDN_CONV, LANES), lambda bi, c: (c, 0, 0)),
        ],
        out_specs=pl.BlockSpec((1, l, LANES), lambda bi, c: (bi, 0, c)),
        out_shape=jax.ShapeDtypeStruct((b, l, 3 * GDN_W), jnp.float32),
        compiler_params=_cparams(("parallel", "parallel"), 32),
        name="gdn_conv",
    )(p, w)


def _lane_blocks(cat, n):
    shift = int(math.log2(cat.shape[1] // n))
    blk = jnp.right_shift(lax.broadcasted_iota(jnp.int32, cat.shape, 1), shift)
    return [jnp.where(blk == b, cat, 0.0) for b in range(n)]


def _block_diag_split(cat, n):
    hi = cat.astype(MXU_DTYPE).astype(jnp.float32)
    lo = cat - hi
    return tuple(jnp.concatenate(_lane_blocks(x, n), axis=0).astype(MXU_DTYPE) for x in (hi, lo))


def _unit_lower_inverse(lms, n):
    c = lms[0].shape[0]
    row = lax.broadcasted_iota(jnp.int32, (c, n * c), 0)
    lane = lax.broadcasted_iota(jnp.int32, (c, n * c), 1)
    eye = (jnp.bitwise_and(lane, c - 1) == row).astype(jnp.float32)
    ps = [-jnp.concatenate(lms[g:g + n], axis=1) for g in range(0, len(lms), n)]
    ts = [eye + p for p in ps]
    pbds = [_block_diag_split(p, n) for p in ps]
    for _ in range(int(math.log2(c)) - 1):
        ps = [_mm_x3(_split(p), pbd) for p, pbd in zip(ps, pbds)]
        pbds = [_block_diag_split(p, n) for p in ps]
        ts = [t + _mm_x3(_split(t), pbd) for t, pbd in zip(ts, pbds)]
    return ts


def _gdn_chunk_kernel(q_ref, k_ref, v_ref, ba_ref, a_ref, dt_ref, s0_ref,
                      o_ref, sout_ref, s_ref):
    d = pl.program_id(0)
    n = pl.program_id(2)
    c = GDN_CHUNK
    nb = q_ref.shape[0]

    @pl.when(n == 0)
    def _():
        for bb in range(nb):
            s_ref[bb * GDN_HEADS:(bb + 1) * GDN_HEADS] = s0_ref[0, bb]

    ii = lax.broadcasted_iota(jnp.int32, (c, c), 0)
    jj = lax.broadcasted_iota(jnp.int32, (c, c), 1)
    sgn = 1 - 2 * d
    diff = (ii - jj) * sgn
    incl = diff >= 0
    strict = diff > 0
    tri = incl.astype(jnp.float32)

    beta_alls, gc_alls, gc_all_ts, eg_alls, er_alls, el_alls = [], [], [], [], [], []
    for bb in range(nb):
        ba = ba_ref[bb]
        g_all = -a_ref[0] * _softplus(ba + dt_ref[0])
        gc_all = _mm_f32(tri, g_all)
        gtot_all = jnp.sum(g_all, axis=0, keepdims=True)
        beta_alls.append(_sigmoid(ba))
        gc_alls.append(gc_all)
        gc_all_ts.append(gc_all.T)
        eg_alls.append(jnp.exp(gc_all))
        er_alls.append(jnp.exp(gtot_all - gc_all))
        el_alls.append(jnp.exp(gtot_all))

    units = [(bb, h) for bb in range(nb) for h in range(GDN_HEADS)]

    def col(tables, bb, lane):
        return tables[bb][:, lane:lane + 1]

    sls = [slice(h * GDN_DK, (h + 1) * GDN_DK) for _, h in units]
    ks = [k_ref[bb, :, sl] for (bb, _), sl in zip(units, sls)]
    kbs = [k * col(beta_alls, bb, h) for (bb, h), k in zip(units, ks)]
    decs = [jnp.exp(jnp.where(incl, col(gc_alls, bb, GDN_HEADS + h)
                              - gc_all_ts[bb][GDN_HEADS + h:GDN_HEADS + h + 1, :], -jnp.inf))
            for bb, h in units]
    lms = [jnp.where(strict, _mm_nt(kb, k) * dec, 0.0) for kb, k, dec in zip(kbs, ks, decs)]
    us, ws = [], []
    tinvs = _unit_lower_inverse(lms, GDN_PACK)
    for g0, tinv in zip(range(0, len(units), GDN_PACK), tinvs):
        grp = range(g0, g0 + GDN_PACK)
        vb = jnp.concatenate([v_ref[units[x][0], :, sls[x]] * col(beta_alls, *units[x])
                              for x in grp], axis=0)
        kg = jnp.concatenate([kbs[x] * col(eg_alls, units[x][0], GDN_HEADS + units[x][1])
                              for x in grp], axis=0)
        for t_x in _lane_blocks(tinv, GDN_PACK):
            us.append(_mm(t_x, vb))
            ws.append(_mm(t_x, kg))
    qs = [q_ref[bb, :, sl] for (bb, _), sl in zip(units, sls)]
    attns = [jnp.where(incl, _mm_nt(q, k) * dec, 0.0) for q, k, dec in zip(qs, ks, decs)]
    ss = [s_ref[bb * GDN_HEADS + h] for bb, h in units]
    v_news = [u - _mm(w, s) for u, w, s in zip(us, ws, ss)]
    for x, (bb, h) in enumerate(units):
        qg = qs[x] * col(eg_alls, bb, GDN_HEADS + h)
        o_ref[0, bb, :, sls[x]] = _mm(qg, ss[x]) + _mm(attns[x], v_news[x])
    for x, (bb, h) in enumerate(units):
        kd = ks[x] * col(er_alls, bb, GDN_HEADS + h)
        s_ref[bb * GDN_HEADS + h] = (ss[x] * col(el_alls, bb, GDN_HEADS + h)
                                     + _mm_tn(kd, v_news[x]))

    @pl.when(n == pl.num_programs(2) - 1)
    def _():
        for bb in range(nb):
            sout_ref[0, bb] = s_ref[bb * GDN_HEADS:(bb + 1) * GDN_HEADS]


def gdn_chunks(qkv, p, ba_tile0, a_rows, dt_rows, s0, *, nb):
    b, l, _ = qkv.shape
    nc = l // GDN_CHUNK
    nb = math.gcd(nb, b)

    def row(d, bi, n):
        return n + d * (nc - 1 - 2 * n)

    return pl.pallas_call(
        _gdn_chunk_kernel,
        grid=(2, b // nb, nc),
        in_specs=[
            pl.BlockSpec((nb, GDN_CHUNK, GDN_W), lambda d, bi, n: (bi, row(d, bi, n), 0)),
            pl.BlockSpec((nb, GDN_CHUNK, GDN_W), lambda d, bi, n: (bi, row(d, bi, n), 1)),
            pl.BlockSpec((nb, GDN_CHUNK, GDN_W), lambda d, bi, n: (bi, row(d, bi, n), 2)),
            pl.BlockSpec((nb, GDN_CHUNK, LANES), lambda d, bi, n: (bi, row(d, bi, n), ba_tile0 + d)),
            pl.BlockSpec((1, 1, LANES), lambda d, bi, n: (d, 0, 0)),
            pl.BlockSpec((1, 1, LANES), lambda d, bi, n: (d, 0, 0)),
            pl.BlockSpec((1, nb, GDN_HEADS, GDN_DK, GDN_DV), lambda d, bi, n: (d, bi, 0, 0, 0)),
        ],
        out_specs=[
            pl.BlockSpec((1, nb, GDN_CHUNK, GDN_W), lambda d, bi, n: (d, bi, row(d, bi, n), 0)),
            pl.BlockSpec((1, nb, GDN_HEADS, GDN_DK, GDN_DV), lambda d, bi, n: (d, bi, 0, 0, 0)),
        ],
        out_shape=[
            jax.ShapeDtypeStruct((2, b, l, GDN_W), jnp.float32),
            jax.ShapeDtypeStruct((2, b, GDN_HEADS, GDN_DK, GDN_DV), jnp.float32),
        ],
        scratch_shapes=[pltpu.VMEM((nb * GDN_HEADS, GDN_DK, GDN_DV), jnp.float32)],
        compiler_params=_cparams(("parallel", "parallel", "arbitrary"), 40),
        name="gdn_chunks",
    )(qkv, qkv, qkv, p, a_rows, dt_rows, s0)


def _gated_out_kernel(of_ref, ob_ref, z_ref, w_ref, y_ref):
    for h in range(GDN_HEADS):
        sl = slice(h * GDN_DV, (h + 1) * GDN_DV)
        o = of_ref[0, 0, :, sl] + ob_ref[0, 0, :, sl]
        y = o * lax.rsqrt(jnp.mean(o * o, axis=-1, keepdims=True) + EPS)
        y_ref[0, :, sl] = (y * w_ref[...] * _silu(z_ref[0, :, sl])).astype(y_ref.dtype)


def gated_out(o, p, z_tile, w_norm, *, tm):
    _, b, l, _ = o.shape
    tm = min(tm, l)
    return pl.pallas_call(
        _gated_out_kernel,
        grid=(b, l // tm),
        in_specs=[
            pl.BlockSpec((1, 1, tm, GDN_W), lambda bi, i: (0, bi, i, 0)),
            pl.BlockSpec((1, 1, tm, GDN_W), lambda bi, i: (1, bi, i, 0)),
            pl.BlockSpec((1, tm, GDN_W), lambda bi, i: (bi, i, z_tile)),
            pl.BlockSpec((1, GDN_DV), lambda bi, i: (0, 0)),
        ],
        out_specs=pl.BlockSpec((1, tm, GDN_W), lambda bi, i: (bi, i, 0)),
        out_shape=jax.ShapeDtypeStruct((b, l, GDN_W), MXU_DTYPE),
        compiler_params=_cparams(("parallel", "parallel"), 40),
        name="gated_out",
    )(o, o, p, w_norm.reshape(1, GDN_DV))


def _fourier_kernel(x_ref, bd_ref, cs_ref, o_ref, z_ref):
    i = pl.program_id(1)
    l = x_ref.shape[1]

    @pl.when(i == 0)
    def _():
        z = _mm(x_ref[0], bd_ref[...])
        z_ref[0:l, :] = z[:, :FNET_W].astype(z_ref.dtype)
        z_ref[l:2 * l, :] = z[:, FNET_W:].astype(z_ref.dtype)

    o_ref[0] = jnp.dot(cs_ref[...], z_ref[...],
                       preferred_element_type=jnp.float32).astype(o_ref.dtype)


def _dft_tables(l):
    def cs(n, scale):
        k = jnp.arange(n, dtype=jnp.int32)
        ang = ((k[:, None] * k[None, :]) % n).astype(jnp.float32) * (2.0 * math.pi / n)
        return jnp.cos(ang) * scale, jnp.sin(ang) * scale

    c_l, s_l = cs(l, 1.0)
    c_g, s_g = cs(FNET_GW, 1.0 / math.sqrt(l * FNET_GW))
    eye = jnp.eye(FNET_GROUPS, dtype=jnp.float32)
    bd = jnp.concatenate([jnp.kron(eye, c_g), jnp.kron(eye, s_g)], axis=1)
    cs_l = jnp.concatenate([c_l, -s_l], axis=1)
    return bd.astype(MXU_DTYPE), cs_l.astype(MXU_DTYPE)


def fourier_mix(p, f_tile, *, tm):
    b, l, _ = p.shape
    tm = min(tm, l)
    bd, cs_l = _dft_tables(l)
    return pl.pallas_call(
        _fourier_kernel,
        grid=(b, l // tm),
        in_specs=[
            pl.BlockSpec((1, l, FNET_W), lambda bi, i: (bi, 0, f_tile)),
            pl.BlockSpec((FNET_W, 2 * FNET_W), lambda bi, i: (0, 0)),
            pl.BlockSpec((tm, 2 * l), lambda bi, i: (i, 0)),
        ],
        out_specs=pl.BlockSpec((1, tm, FNET_W), lambda bi, i: (bi, i, 0)),
        out_shape=jax.ShapeDtypeStruct((b, l, FNET_W), MXU_DTYPE),
        scratch_shapes=[pltpu.VMEM((2 * l, FNET_W), MXU_DTYPE)],
        compiler_params=_cparams(("parallel", "arbitrary"), 40),
        name="fourier_mix",
    )(p, bd, cs_l)


def _swa_kernel(*refs, use_window):
    if use_window:
        sink_ref, q_ref, k_ref, v_ref, kc_ref, vc_ref, o_ref = refs
    else:
        sink_ref, q_ref, kc_ref, vc_ref, o_ref = refs
    i = pl.program_id(1)
    tq = q_ref.shape[1]
    scale = SWA_DH ** -0.5
    if use_window:
        l = k_ref.shape[1]
        span = SWA_BLOCK + 2 * SWA_WINDOW
        start = i * tq
        base = pl.multiple_of(jnp.clip(start - SWA_WINDOW, 0, l - span), SWA_BLOCK)
        qpos = start + lax.broadcasted_iota(jnp.int32, (tq, span), 0)
        kpos = base + lax.broadcasted_iota(jnp.int32, (tq, span), 1)
        band = jnp.abs(qpos - kpos) <= SWA_WINDOW
    for g in range(SWA_KV):
        ksl = slice(g * SWA_DH, (g + 1) * SWA_DH)
        qg = jnp.concatenate(
            [q_ref[0, :, (g * SWA_GROUP + hh) * SWA_DH:(g * SWA_GROUP + hh + 1) * SWA_DH]
             for hh in range(SWA_GROUP)], axis=0)
        kc = kc_ref[0, :, ksl]
        vc = vc_ref[0, :, ksl]
        lc_all = _mm_nt(qg, kc) * scale
        if use_window:
            kw = k_ref[0, pl.ds(base, span), ksl]
            vw = v_ref[0, pl.ds(base, span), ksl]
            lw_all = _mm_nt(qg, kw) * scale
        outs = []
        for hh in range(SWA_GROUP):
            rs = slice(hh * tq, (hh + 1) * tq)
            sink = sink_ref[g * SWA_GROUP + hh]
            lc = lc_all[rs]
            m = jnp.maximum(jnp.max(lc, axis=-1, keepdims=True), sink)
            if use_window:
                lw = jnp.where(band, lw_all[rs], -jnp.inf)
                m = jnp.maximum(m, jnp.max(lw, axis=-1, keepdims=True))
                pw = jnp.exp(lw - m)
            pc = jnp.exp(lc - m)
            den = jnp.sum(pc, axis=-1, keepdims=True) + jnp.exp(sink - m)
            acc = _mm(pc, vc)
            if use_window:
                den = den + jnp.sum(pw, axis=-1, keepdims=True)
                acc = acc + _mm(pw, vw)
            outs.append(acc / den)
        for pair in range(SWA_GROUP // 2):
            lo = (g * SWA_GROUP + 2 * pair) * SWA_DH
            o_ref[0, :, lo:lo + 2 * SWA_DH] = jnp.concatenate(
                [outs[2 * pair], outs[2 * pair + 1]], axis=1).astype(o_ref.dtype)


def swa_attention(qkv, ckv, sinks, *, use_window):
    b, l, _ = qkv.shape
    c = ckv.shape[1]
    tq = SWA_BLOCK
    kt = SWA_Q_W // SWA_KV_W
    in_specs = [
        pl.BlockSpec(memory_space=pltpu.SMEM),
        pl.BlockSpec((1, tq, SWA_Q_W), lambda bi, i: (bi, i, 0)),
    ]
    args = [sinks, qkv]
    if use_window:
        in_specs += [
            pl.BlockSpec((1, l, SWA_KV_W), lambda bi, i: (bi, 0, kt)),
            pl.BlockSpec((1, l, SWA_KV_W), lambda bi, i: (bi, 0, kt + 1)),
        ]
        args += [qkv, qkv]
    in_specs += [
        pl.BlockSpec((1, c, SWA_KV_W), lambda bi, i: (bi, 0, 0)),
        pl.BlockSpec((1, c, SWA_KV_W), lambda bi, i: (bi, 0, 1)),
    ]
    args += [ckv, ckv]
    return pl.pallas_call(
        functools.partial(_swa_kernel, use_window=use_window),
        grid=(b, l // tq),
        in_specs=in_specs,
        out_specs=pl.BlockSpec((1, tq, SWA_Q_W), lambda bi, i: (bi, i, 0)),
        out_shape=jax.ShapeDtypeStruct((b, l, SWA_Q_W), MXU_DTYPE),
        compiler_params=_cparams(("parallel", "parallel"), 40),
        name="swa_attention",
    )(*args)


def _rope_tables(l):
    rows = l // GRID_W
    r = jnp.broadcast_to(jnp.arange(rows)[:, None], (rows, GRID_W)).reshape(-1).astype(jnp.float32)
    col = jnp.broadcast_to(jnp.arange(GRID_W)[None, :], (rows, GRID_W)).reshape(-1).astype(jnp.float32)
    n = SWA_DH // 4
    freq = ROPE_BASE ** (-jnp.arange(n, dtype=jnp.float32) / n)
    ang_r = r[:, None] * freq
    ang_c = col[:, None] * freq
    cos = jnp.concatenate([jnp.cos(ang_r)] * 2 + [jnp.cos(ang_c)] * 2, axis=-1)
    sin = jnp.concatenate([-jnp.sin(ang_r), jnp.sin(ang_r), -jnp.sin(ang_c), jnp.sin(ang_c)], axis=-1)
    reps = LANES // SWA_DH
    return jnp.tile(cos, (1, reps)), jnp.tile(sin, (1, reps))


def _sort_desc(rows):
    n = len(rows)
    assert n & (n - 1) == 0
    rows = list(rows)
    k = 2
    while k <= n:
        j = k // 2
        while j >= 1:
            for i in range(n):
                p = i ^ j
                if p > i:
                    hi, lo = jnp.maximum(rows[i], rows[p]), jnp.minimum(rows[i], rows[p])
                    rows[i], rows[p] = (hi, lo) if (i & k) == 0 else (lo, hi)
            j //= 2
        k *= 2
    return rows


def _top_values(s, count):
    levels = _sort_desc([s[SUBLANES * l:SUBLANES * (l + 1)] for l in range(s.shape[0] // SUBLANES)])
    levels = levels[:count]
    vals = []
    for k in range(count):
        m = jnp.max(levels[0], axis=0, keepdims=True)
        vals.append(m)
        hit = levels[0] == m
        keep = max(count - k - 1, 1)
        below = levels[1:] + [jnp.full_like(levels[0], -jnp.inf)]
        levels = [jnp.where(hit, b, a) for a, b in zip(levels[:keep], below[:keep])]
    return vals


def _peer_cells():
    n = PEER_TOPK + 1
    return [(a, b) for a in range(n) for b in range(n) if (a + 1) * (b + 1) <= n]


def _peer_topk_kernel(q_ref, keys_ref, s1_ref, e1_ref, d_ref, e0_ref):
    n = PEER_TOPK + 1
    cells = _peer_cells()
    for h in range(PEER_HEADS):
        q0 = q_ref[:, (2 * h) * PEER_HALF:(2 * h + 1) * PEER_HALF]
        q1 = q_ref[:, (2 * h + 1) * PEER_HALF:(2 * h + 2) * PEER_HALF]
        s0 = lax.dot_general(keys_ref[0], q0, _NT, precision=_HI,
                             preferred_element_type=jnp.float32)
        s1 = lax.dot_general(keys_ref[1], q1, _NT, precision=_HI,
                             preferred_element_type=jnp.float32)
        top0 = _top_values(s0, n)
        top1 = _top_values(s1, n)
        rows = [top0[a] + top1[b] for a, b in cells]
        rows += [jnp.full_like(rows[0], -jnp.inf)] * (-len(rows) % (8 * SUBLANES))
        best = _top_values(jnp.concatenate(rows, axis=0), n)
        thr = 0.5 * (best[PEER_TOPK - 1] + best[PEER_TOPK])
        z = jnp.ones_like(thr)
        for kk in range(1, PEER_TOPK):
            z = z + jnp.exp(best[kk] - best[0])
        s1_ref[h] = s1
        e1_ref[h] = jnp.exp(s1 - top1[0]) / z
        d_ref[h] = thr - s0
        e0_ref[h] = jnp.exp(s0 - top0[0])


def peer_topk(q, keys, *, tt):
    t = q.shape[0]
    tt = min(tt, t)
    out = jax.ShapeDtypeStruct((PEER_HEADS, PEER_NKEYS, t), jnp.float32)
    ospec = pl.BlockSpec((PEER_HEADS, PEER_NKEYS, tt), lambda i: (0, 0, i))
    return pl.pallas_call(
        _peer_topk_kernel,
        grid=(t // tt,),
        in_specs=[
            pl.BlockSpec((tt, q.shape[1]), lambda i: (i, 0)),
            pl.BlockSpec((2, PEER_NKEYS, PEER_HALF), lambda i: (0, 0, 0)),
        ],
        out_specs=[ospec] * 4,
        out_shape=[out] * 4,
        compiler_params=_cparams(("parallel",), 40),
        name="peer_topk",
    )(q, keys)


def _gelu(x):
    return 0.5 * x * (1.0 + lax.erf(x * (2.0 ** -0.5)))


def _peer_dense_kernel(*refs, final_norm):
    (hmt_ref, u0_ref, ua_ref, ub_ref, vt_ref, s1_ref, e1_ref, d_ref, e0_ref, x_ref,
     gate_ref) = refs[:11]
    pos = 11
    if final_norm:
        fn_ref = refs[pos]
        pos += 1
    o_ref, acc_ref, pt_ref, at_ref = refs[pos:pos + 4]
    e = pl.program_id(1)
    eg, tm = at_ref.shape
    ni = eg // PEER_NKEYS

    def first_matmul(u_ref):
        return jnp.dot(u_ref[...], hmt_ref[...], preferred_element_type=jnp.float32)

    @pl.when(e == 0)
    def _():
        acc_ref[...] = jnp.zeros_like(acc_ref)
        at_ref[...] = first_matmul(u0_ref)

    def gate_group(r, a_block):
        for il in range(ni):
            ig = r * ni + il
            rs = slice(ig * PEER_NKEYS, (ig + 1) * PEER_NKEYS)
            for tl in range(tm // LANES):
                cs = slice(tl * LANES, (tl + 1) * LANES)
                g = None
                for h in range(PEER_HEADS):
                    drow = d_ref[h, ig:ig + 1, cs]
                    erow = e0_ref[h, ig:ig + 1, cs]
                    term = jnp.where(s1_ref[h, :, cs] >= drow, e1_ref[h, :, cs] * erow, 0.0)
                    g = term if g is None else g + term
                pt_ref[rs, cs] = (g * _gelu(a_block(il, cs))).astype(pt_ref.dtype)

    def second_matmul(r):
        es = slice(r * eg, (r + 1) * eg)
        acc_ref[...] += jnp.dot(vt_ref[:, es], pt_ref[es, :], preferred_element_type=jnp.float32)

    a1 = first_matmul(ua_ref)
    gate_group(0, lambda il, cs: at_ref[il * PEER_NKEYS:(il + 1) * PEER_NKEYS, cs])
    second_matmul(0)
    at_ref[...] = first_matmul(ub_ref)
    gate_group(1, lambda il, cs: a1[il * PEER_NKEYS:(il + 1) * PEER_NKEYS, cs])
    second_matmul(1)

    @pl.when(e == pl.num_programs(1) - 1)
    def _():
        y = x_ref[...] + gate_ref[0] * acc_ref[...].T
        if final_norm:
            y = y * lax.rsqrt(jnp.mean(y * y, axis=-1, keepdims=True) + EPS) * fn_ref[...]
        o_ref[...] = y


def peer_dense(hmt, u, vt, layer, sel, x, gate, rows_per_gate, final_gain=None, *, tm, eg):
    t, d = x.shape
    ne = u.shape[1]
    tm = min(tm, t, rows_per_gate)
    te = 2 * eg
    assert rows_per_gate % tm == 0 and t % tm == 0 and ne % te == 0 and eg % PEER_NKEYS == 0
    ni = te // PEER_NKEYS
    assert ni % SUBLANES == 0
    s1, e1, dd, e0 = sel
    gpb = rows_per_gate // tm
    last = ne // eg - 1
    once = pl.Buffered(1)
    full = pl.BlockSpec((PEER_HEADS, PEER_NKEYS, tm), lambda i, e: (0, 0, i))
    rowsp = pl.BlockSpec((PEER_HEADS, ni, tm), lambda i, e: (0, e, i))
    in_specs = [
        pl.BlockSpec((d, tm), lambda i, e: (0, i)),
        pl.BlockSpec((None, eg, d), lambda i, e: (layer, 0, 0), pipeline_mode=once),
        pl.BlockSpec((None, eg, d), lambda i, e: (layer, 2 * e + 1, 0)),
        pl.BlockSpec((None, eg, d), lambda i, e: (layer, jnp.minimum(2 * e + 2, last), 0)),
        pl.BlockSpec((None, d, te), lambda i, e: (layer, 0, e)),
        full, full, rowsp, rowsp,
        pl.BlockSpec((tm, d), lambda i, e: (i, 0), pipeline_mode=once),
        pl.BlockSpec((1, 1, d), lambda i, e: (i // gpb, 0, 0)),
    ]
    args = [hmt, u, u, u, vt, s1, e1, dd, e0, x, gate.reshape(gate.shape[0], 1, d)]
    if final_gain is not None:
        in_specs.append(pl.BlockSpec((1, d), lambda i, e: (0, 0)))
        args.append(final_gain.reshape(1, d))
    return pl.pallas_call(
        functools.partial(_peer_dense_kernel, final_norm=final_gain is not None),
        grid=(t // tm, ne // te),
        in_specs=in_specs,
        out_specs=pl.BlockSpec((tm, d), lambda i, e: (i, 0)),
        out_shape=jax.ShapeDtypeStruct((t, d), jnp.float32),
        scratch_shapes=[pltpu.VMEM((d, tm), jnp.float32), pltpu.VMEM((te, tm), MXU_DTYPE),
                        pltpu.VMEM((eg, tm), jnp.float32)],
        compiler_params=_cparams(("arbitrary", "arbitrary"), 60),
        name="peer_dense",
    )(*args)


def peer_block(x, gain, shift, scale, gate, w_q, keys, u, vt, layer, final_gain=None):
    b, l, d = x.shape
    q, hmt = mod_matmul(x, gain, shift, scale, w_q, tm=1024, tn=512, emit_a=True)
    sel = peer_topk(q.reshape(b * l, d), keys, tt=256)
    out = peer_dense(hmt, u, vt, layer, sel, x.reshape(b * l, d), gate, l,
                     final_gain, tm=256, eg=512)
    return out.reshape(b, l, d)


def _pack_even_w_in(w_in):
    d = w_in.shape[0]
    o = 4 * GDN_W
    h = GDN_HEADS
    ba = w_in[:, o:o + 4 * h]
    zeros = jnp.zeros((d, LANES - 2 * h), w_in.dtype)
    fwd = jnp.concatenate([ba[:, 0:h], ba[:, 2 * h:3 * h], zeros], axis=1)
    bwd = jnp.concatenate([ba[:, h:2 * h], ba[:, 3 * h:4 * h], zeros], axis=1)
    pad = jnp.zeros((d, 2 * LANES), w_in.dtype)
    packed = jnp.concatenate([w_in[:, :o], w_in[:, o + 4 * h:], fwd, bwd, pad], axis=1)
    return packed.astype(MXU_DTYPE)


def _lane_rows(vals):
    z = jnp.zeros((2, GDN_HEADS), jnp.float32)
    pad = jnp.zeros((2, LANES - 2 * GDN_HEADS), jnp.float32)
    return jnp.concatenate([z, vals.astype(jnp.float32), pad], axis=1).reshape(2, 1, LANES)


def even_layer_mixer(x, h_c, m_lat, m_ctx, gain, w_in, conv_w, a_log, dt_bias, out_norm, w_out,
                     ctx_out):
    b = x.shape[0]
    w_packed = _pack_even_w_in(w_in)
    ba_tile0 = (4 * GDN_W + FNET_W) // LANES
    z_tile = 3
    f_tile = 4 * GDN_W // FNET_W
    a_rows = _lane_rows(jnp.exp(a_log.astype(jnp.float32)))
    dt_rows = _lane_rows(dt_bias)
    w_out_c = w_out.astype(MXU_DTYPE)
    w_y, w_f = w_out_c[:GDN_W], w_out_c[GDN_W:]

    p_ctx = mod_matmul(h_c, gain, m_ctx[0], m_ctx[1], w_packed, tm=512, tn=512)
    p_lat = mod_matmul(x, gain, m_lat[0], m_lat[1], w_packed, tm=1024, tn=512)
    zero = jnp.zeros((2, b, GDN_HEADS, GDN_DK, GDN_DV), jnp.float32)
    o_ctx, s_ctx = gdn_chunks(gdn_short_conv(p_ctx, conv_w), p_ctx, ba_tile0, a_rows, dt_rows, zero,
                              nb=2)
    o_lat, _ = gdn_chunks(gdn_short_conv(p_lat, conv_w), p_lat, ba_tile0, a_rows, dt_rows, s_ctx,
                          nb=2)

    def finish(p, o, resid, gate):
        y = gated_out(o, p, z_tile, out_norm, tm=512)
        f = fourier_mix(p, f_tile, tm=512)
        return resid_matmul([y, f], [w_y, w_f], resid, gate, tm=1024, tn=512)

    x = finish(p_lat, o_lat, x, m_lat[2])
    if ctx_out:
        h_c = finish(p_ctx, o_ctx, h_c, m_ctx[2])
    return x, h_c


def odd_layer_mixer(x, h_c, m_lat, m_ctx, gain, w_qkv, sinks, w_out, ctx_out):
    l = x.shape[1]
    w_c = w_qkv.astype(MXU_DTYPE)
    w_out_c = w_out.astype(MXU_DTYPE)
    rope = _rope_tables(l)
    tn = 256
    qkv = mod_matmul(x, gain, m_lat[0], m_lat[1], w_c, tm=1024, tn=tn,
                     rope=rope, rope_tiles=(SWA_Q_W + SWA_KV_W) // tn)
    ckv = mod_matmul(h_c, gain, m_ctx[0], m_ctx[1], w_c[:, SWA_Q_W:], tm=512, tn=tn)
    sinks = sinks.astype(jnp.float32)
    o = swa_attention(qkv, ckv, sinks, use_window=True)
    x_new = resid_matmul([o], [w_out_c], x, m_lat[2], tm=1024, tn=512)
    if ctx_out:
        q_c = mod_matmul(h_c, gain, m_ctx[0], m_ctx[1], w_c[:, :SWA_Q_W], tm=512, tn=tn)
        o_c = swa_attention(q_c, ckv, sinks, use_window=False)
        h_c = resid_matmul([o_c], [w_out_c], h_c, m_ctx[2], tm=1024, tn=512)
    return x_new, h_c


def kernel(x, c, ctx, c_ctx, ada_w, ada_b, norm_mix, norm_ffn, even_w_in, gdn_conv, gdn_a_log, gdn_dt_bias, gdn_out_norm, even_w_out, odd_w_qkv, odd_sinks, odd_w_out, peer_w_q, peer_keys, peer_u, peer_v, final_norm):
    b, _, d = x.shape
    depth = ada_w.shape[0]
    rows = 16
    cond = jnp.concatenate([c, c_ctx[None, :], jnp.zeros((rows - b - 1, d), c.dtype)], axis=0)
    mods = ada_mod_all(cond, ada_w, ada_b)
    conv_w_all = gdn_conv
    u_all = peer_u.astype(MXU_DTYPE)
    vt_all = jnp.swapaxes(peer_v.astype(MXU_DTYPE), 1, 2)
    h_c = ctx
    for i in range(depth):
        last = i == depth - 1
        j = i // 2
        m = mods[i].reshape(rows, N_MOD, d)
        m_lat = [m[:b, k] for k in range(N_MOD)]
        m_ctx = [jnp.broadcast_to(m[b, k][None, :], (b, d)) for k in range(N_MOD)]
        if i % 2 == 0:
            x, h_c = even_layer_mixer(x, h_c, m_lat, m_ctx, norm_mix[i], even_w_in[j], conv_w_all[j],
                                      gdn_a_log[j], gdn_dt_bias[j], gdn_out_norm[j], even_w_out[j],
                                      not last)
        else:
            x, h_c = odd_layer_mixer(x, h_c, m_lat, m_ctx, norm_mix[i], odd_w_qkv[j], odd_sinks[j],
                                     odd_w_out[j], not last)
        w_q = peer_w_q[i].astype(MXU_DTYPE)
        keys = peer_keys[i].astype(jnp.float32)
        x = peer_block(x, norm_ffn[i], m_lat[3], m_lat[4], m_lat[5], w_q, keys, u_all, vt_all, i,
                       final_norm if last else None)
        if not last:
            h_c = peer_block(h_c, norm_ffn[i], m_ctx[3], m_ctx[4], m_ctx[5], w_q, keys,
                             u_all, vt_all, i)
    return x
```

```python
import functools
import math

import jax
import jax.numpy as jnp
from jax import lax
from jax.experimental import pallas as pl
from jax.experimental.pallas import tpu as pltpu

N_MOD = 6
EPS = 1e-6
GDN_HEADS = 12
GDN_DK = 128
GDN_DV = 128
GDN_W = GDN_HEADS * GDN_DV
GDN_CONV = 5
GDN_CHUNK = 64
GDN_PACK = 2
FNET_GROUPS = 4
FNET_GW = 128
FNET_W = FNET_GROUPS * FNET_GW
GRID_W = 64
SWA_HEADS = 32
SWA_KV = 4
SWA_GROUP = SWA_HEADS // SWA_KV
SWA_DH = 64
SWA_WINDOW = 128
SWA_BLOCK = 128
ROPE_BASE = 10000.0
SWA_Q_W = SWA_HEADS * SWA_DH
SWA_KV_W = SWA_KV * SWA_DH
PEER_HEADS = 8
PEER_NKEYS = 128
PEER_HALF = 128
PEER_TOPK = 16
PEER_EXPERT_GROUP = 512

LANES = 128
SUBLANES = 8
VMEM_BYTES_V7X = 64 * 1024 * 1024
MXU_DTYPE = jnp.bfloat16

_HI = lax.Precision.HIGHEST
_NT = (((1,), (1,)), ((), ()))
_TN = (((0,), (0,)), ((), ()))


def _cparams(semantics, vmem_mb):
    assert vmem_mb * 1024 * 1024 < VMEM_BYTES_V7X
    return pltpu.CompilerParams(dimension_semantics=semantics,
                                vmem_limit_bytes=vmem_mb * 1024 * 1024)


def _mm(a, b):
    return jnp.dot(a.astype(MXU_DTYPE), b.astype(MXU_DTYPE),
                   preferred_element_type=jnp.float32)


def _mm_nt(a, b):
    return lax.dot_general(a.astype(MXU_DTYPE), b.astype(MXU_DTYPE), _NT,
                           preferred_element_type=jnp.float32)


def _mm_tn(a, b):
    return lax.dot_general(a.astype(MXU_DTYPE), b.astype(MXU_DTYPE), _TN,
                           preferred_element_type=jnp.float32)


def _mm_f32(a, b):
    return jnp.dot(a, b, precision=_HI, preferred_element_type=jnp.float32)


def _split(a):
    hi = a.astype(MXU_DTYPE)
    lo = (a - hi.astype(jnp.float32)).astype(MXU_DTYPE)
    return hi, lo


def _mm_x3(a, b):
    (ah, al), (bh, bl) = a, b
    dot = functools.partial(jnp.dot, preferred_element_type=jnp.float32)
    m = ah.shape[0]
    both = dot(jnp.concatenate([ah, al], axis=0), bh)
    return both[:m] + (dot(ah, bl) + both[m:])


def _sigmoid(x):
    return 1.0 / (1.0 + jnp.exp(-x))


def _silu(x):
    return x * _sigmoid(x)


def _softplus(x):
    return jnp.maximum(x, 0.0) + jnp.log1p(jnp.exp(-jnp.abs(x)))


def _ada_kernel(cond_ref, w_ref, b_ref, o_ref):
    a = _silu(cond_ref[...])
    o_ref[0] = _mm(a, w_ref[0]) + b_ref[0]


def ada_mod_all(cond, ada_w, ada_b, *, tn=1024):
    depth, d, n = ada_w.shape
    r = cond.shape[0]
    return pl.pallas_call(
        _ada_kernel,
        grid=(depth, n // tn),
        in_specs=[
            pl.BlockSpec((r, d), lambda l, j: (0, 0)),
            pl.BlockSpec((1, d, tn), lambda l, j: (l, 0, j)),
            pl.BlockSpec((1, 1, tn), lambda l, j: (l, 0, j)),
        ],
        out_specs=pl.BlockSpec((1, r, tn), lambda l, j: (l, 0, j)),
        out_shape=jax.ShapeDtypeStruct((depth, r, n), jnp.float32),
        compiler_params=_cparams(("parallel", "parallel"), 40),
        name="ada_mod",
    )(cond, ada_w, ada_b.reshape(depth, 1, n))


def _rope_swap(y):
    n = y.shape[-1]
    lane = lax.broadcasted_iota(jnp.int32, y.shape, 1)
    up = pltpu.roll(y, n - 16, 1)
    down = pltpu.roll(y, 16, 1)
    return jnp.where(lane % 32 < 16, up, down)


def _mod_matmul_kernel(*refs, rope_tiles, emit_a, tn):
    x_ref, gain_ref, shift_ref, scale_ref, w_ref = refs[:5]
    pos = 5
    if rope_tiles:
        cos_ref, sin_ref = refs[pos:pos + 2]
        pos += 2
    o_ref = refs[pos]
    pos += 1
    if emit_a:
        a_out_ref = refs[pos]
        pos += 1
    a_ref = refs[pos]
    j = pl.program_id(2)

    @pl.when(j == 0)
    def _():
        x = x_ref[0]
        y = x * lax.rsqrt(jnp.mean(x * x, axis=-1, keepdims=True) + EPS)
        y = y * gain_ref[...]
        a = y * (1.0 + scale_ref[0]) + shift_ref[0]
        a_ref[...] = a.astype(a_ref.dtype)
        if emit_a:
            a_out_ref[...] = a.T.astype(a_out_ref.dtype)

    acc = jnp.dot(a_ref[...], w_ref[...], preferred_element_type=jnp.float32)
    if rope_tiles:
        @pl.when(j < rope_tiles)
        def _():
            cos = cos_ref[...]
            sin = sin_ref[...]
            for t in range(tn // LANES):
                sl = slice(t * LANES, (t + 1) * LANES)
                y = acc[:, sl]
                o_ref[0, :, sl] = (y * cos + _rope_swap(y) * sin).astype(o_ref.dtype)

        @pl.when(j >= rope_tiles)
        def _():
            o_ref[0] = acc.astype(o_ref.dtype)
    else:
        o_ref[0] = acc.astype(o_ref.dtype)


def mod_matmul(x, gain, shift, scale, w, *, tm, tn, rope=None, rope_tiles=0, emit_a=False):
    b, l, d = x.shape
    n = w.shape[1]
    tm = min(tm, l)
    assert l % tm == 0 and n % tn == 0 and tn % LANES == 0
    in_specs = [
        pl.BlockSpec((1, tm, d), lambda bi, i, j: (bi, i, 0)),
        pl.BlockSpec((1, d), lambda bi, i, j: (0, 0)),
        pl.BlockSpec((1, 1, d), lambda bi, i, j: (bi, 0, 0)),
        pl.BlockSpec((1, 1, d), lambda bi, i, j: (bi, 0, 0)),
        pl.BlockSpec((d, tn), lambda bi, i, j: (0, j)),
    ]
    args = [x, gain.reshape(1, d), shift.reshape(b, 1, d), scale.reshape(b, 1, d), w]
    if rope_tiles:
        in_specs += [pl.BlockSpec((tm, LANES), lambda bi, i, j: (i, 0))] * 2
        args += list(rope)
    out_specs = [pl.BlockSpec((1, tm, tn), lambda bi, i, j: (bi, i, j))]
    out_shape = [jax.ShapeDtypeStruct((b, l, n), jnp.float32)]
    if emit_a:
        nb = l // tm
        out_specs.append(pl.BlockSpec((d, tm), lambda bi, i, j: (0, bi * nb + i)))
        out_shape.append(jax.ShapeDtypeStruct((d, b * l), w.dtype))
    res = pl.pallas_call(
        functools.partial(_mod_matmul_kernel, rope_tiles=rope_tiles, emit_a=emit_a, tn=tn),
        grid=(b, l // tm, n // tn),
        in_specs=in_specs,
        out_specs=out_specs,
        out_shape=out_shape,
        scratch_shapes=[pltpu.VMEM((tm, d), w.dtype)],
        compiler_params=_cparams(("parallel", "parallel", "arbitrary"), 56),
        name="mod_matmul",
    )(*args)
    return res if emit_a else res[0]


def _resid_matmul_kernel(*refs, n_pairs):
    a_refs = refs[:n_pairs]
    w_refs = refs[n_pairs:2 * n_pairs]
    x_ref, gate_ref, o_ref = refs[2 * n_pairs:]
    acc = jnp.dot(a_refs[0][0], w_refs[0][...], preferred_element_type=jnp.float32)
    for a_ref, w_ref in zip(a_refs[1:], w_refs[1:]):
        acc += jnp.dot(a_ref[0], w_ref[...], preferred_element_type=jnp.float32)
    o_ref[0] = x_ref[0] + gate_ref[0] * acc


def resid_matmul(acts, weights, x, gate, *, tm, tn):
    b, l, n = x.shape
    tm = min(tm, l)
    in_specs = []
    for a in acts:
        in_specs.append(pl.BlockSpec((1, tm, a.shape[2]), lambda bi, i, j: (bi, i, 0)))
    for w in weights:
        in_specs.append(pl.BlockSpec((w.shape[0], tn), lambda bi, i, j: (0, j)))
    in_specs += [
        pl.BlockSpec((1, tm, tn), lambda bi, i, j: (bi, i, j)),
        pl.BlockSpec((1, 1, tn), lambda bi, i, j: (bi, 0, j)),
    ]
    return pl.pallas_call(
        functools.partial(_resid_matmul_kernel, n_pairs=len(acts)),
        grid=(b, l // tm, n // tn),
        in_specs=in_specs,
        out_specs=pl.BlockSpec((1, tm, tn), lambda bi, i, j: (bi, i, j)),
        out_shape=jax.ShapeDtypeStruct((b, l, n), jnp.float32),
        compiler_params=_cparams(("parallel", "parallel", "parallel"), 40),
        name="resid_matmul",
    )(*acts, *weights, x, gate.reshape(b, 1, n))


def _gdn_conv_kernel(p_ref, w_ref, o_ref):
    c = pl.program_id(1)
    x = p_ref[0]
    l = x.shape[0]
    row = lax.broadcasted_iota(jnp.int32, x.shape, 0)
    pad = (GDN_CONV - 1) // 2
    y = x * w_ref[0, pad:pad + 1, :]
    for t in range(GDN_CONV):
        s = t - pad
        if s == 0:
            continue
        xs = pltpu.roll(x, (-s) % l, 0)
        ok = (row + s >= 0) & (row + s < l)
        y = y + jnp.where(ok, xs, 0.0) * w_ref[0, t:t + 1, :]
    y = _silu(y)
    inv = lax.rsqrt(jnp.sum(y * y, axis=-1, keepdims=True) + 1e-6)
    fac = jnp.where(c < 2 * GDN_HEADS, inv, 1.0)
    fac = fac * jnp.where(c < GDN_HEADS, GDN_DK ** -0.5, 1.0)
    o_ref[0] = y * fac


def gdn_short_conv(p, conv_w):
    b, l, _ = p.shape
    nt = 3 * GDN_W // LANES
    w = conv_w.reshape(GDN_CONV, nt, LANES).transpose(1, 0, 2)
    return pl.pallas_call(
        _gdn_conv_kernel,
        grid=(b, nt),
        in_specs=[
            pl.BlockSpec((1, l, LANES), lambda bi, c: (bi, 0, c)),
            pl.BlockSpec((1, GDN_CONV, LANES), lambda bi, c: (c, 0, 0)),
        ],
        out_specs=pl.BlockSpec((1, l, LANES), lambda bi, c: (bi, 0, c)),
        out_shape=jax.ShapeDtypeStruct((b, l, 3 * GDN_W), jnp.float32),
        compiler_params=_cparams(("parallel", "parallel"), 32),
        name="gdn_conv",
    )(p, w)


def _lane_blocks(cat, n):
    shift = int(math.log2(cat.shape[1] // n))
    blk = jnp.right_shift(lax.broadcasted_iota(jnp.int32, cat.shape, 1), shift)
    return [jnp.where(blk == b, cat, 0.0) for b in range(n)]


def _block_diag_split(cat, n):
    hi = cat.astype(MXU_DTYPE).astype(jnp.float32)
    lo = cat - hi
    return tuple(jnp.concatenate(_lane_blocks(x, n), axis=0).astype(MXU_DTYPE) for x in (hi, lo))


def _unit_lower_inverse(lms, n):
    c = lms[0].shape[0]
    row = lax.broadcasted_iota(jnp.int32, (c, n * c), 0)
    lane = lax.broadcasted_iota(jnp.int32, (c, n * c), 1)
    eye = (jnp.bitwise_and(lane, c - 1) == row).astype(jnp.float32)
    ps = [-jnp.concatenate(lms[g:g + n], axis=1) for g in range(0, len(lms), n)]
    ts = [eye + p for p in ps]
    pbds = [_block_diag_split(p, n) for p in ps]
    for _ in range(int(math.log2(c)) - 1):
        ps = [_mm_x3(_split(p), pbd) for p, pbd in zip(ps, pbds)]
        pbds = [_block_diag_split(p, n) for p in ps]
        ts = [t + _mm_x3(_split(t), pbd) for t, pbd in zip(ts, pbds)]
    return ts


def _gdn_chunk_kernel(q_ref, k_ref, v_ref, ba_ref, a_ref, dt_ref, s0_ref,
                      o_ref, sout_ref, s_ref):
    d = pl.program_id(0)
    n = pl.program_id(2)
    c = GDN_CHUNK
    nb = q_ref.shape[0]

    @pl.when(n == 0)
    def _():
        for bb in range(nb):
            s_ref[bb * GDN_HEADS:(bb + 1) * GDN_HEADS] = s0_ref[0, bb]

    ii = lax.broadcasted_iota(jnp.int32, (c, c), 0)
    jj = lax.broadcasted_iota(jnp.int32, (c, c), 1)
    sgn = 1 - 2 * d
    diff = (ii - jj) * sgn
    incl = diff >= 0
    strict = diff > 0
    tri = incl.astype(jnp.float32)

    beta_alls, gc_alls, gc_all_ts, eg_alls, er_alls, el_alls = [], [], [], [], [], []
    for bb in range(nb):
        ba = ba_ref[bb]
        g_all = -a_ref[0] * _softplus(ba + dt_ref[0])
        gc_all = _mm_f32(tri, g_all)
        gtot_all = jnp.sum(g_all, axis=0, keepdims=True)
        beta_alls.append(_sigmoid(ba))
        gc_alls.append(gc_all)
        gc_all_ts.append(gc_all.T)
        eg_alls.append(jnp.exp(gc_all))
        er_alls.append(jnp.exp(gtot_all - gc_all))
        el_alls.append(jnp.exp(gtot_all))

    units = [(bb, h) for bb in range(nb) for h in range(GDN_HEADS)]

    def col(tables, bb, lane):
        return tables[bb][:, lane:lane + 1]

    sls = [slice(h * GDN_DK, (h + 1) * GDN_DK) for _, h in units]
    ks = [k_ref[bb, :, sl] for (bb, _), sl in zip(units, sls)]
    kbs = [k * col(beta_alls, bb, h) for (bb, h), k in zip(units, ks)]
    decs = [jnp.exp(jnp.where(incl, col(gc_alls, bb, GDN_HEADS + h)
                              - gc_all_ts[bb][GDN_HEADS + h:GDN_HEADS + h + 1, :], -jnp.inf))
            for bb, h in units]
    lms = [jnp.where(strict, _mm_nt(kb, k) * dec, 0.0) for kb, k, dec in zip(kbs, ks, decs)]
    us, ws = [], []
    tinvs = _unit_lower_inverse(lms, GDN_PACK)
    for g0, tinv in zip(range(0, len(units), GDN_PACK), tinvs):
        grp = range(g0, g0 + GDN_PACK)
        vb = jnp.concatenate([v_ref[units[x][0], :, sls[x]] * col(beta_alls, *units[x])
                              for x in grp], axis=0)
        kg = jnp.concatenate([kbs[x] * col(eg_alls, units[x][0], GDN_HEADS + units[x][1])
                              for x in grp], axis=0)
        for t_x in _lane_blocks(tinv, GDN_PACK):
            us.append(_mm(t_x, vb))
            ws.append(_mm(t_x, kg))
    qs = [q_ref[bb, :, sl] for (bb, _), sl in zip(units, sls)]
    attns = [jnp.where(incl, _mm_nt(q, k) * dec, 0.0) for q, k, dec in zip(qs, ks, decs)]
    ss = [s_ref[bb * GDN_HEADS + h] for bb, h in units]
    v_news = [u - _mm(w, s) for u, w, s in zip(us, ws, ss)]
    for x, (bb, h) in enumerate(units):
        qg = qs[x] * col(eg_alls, bb, GDN_HEADS + h)
        o_ref[0, bb, :, sls[x]] = _mm(qg, ss[x]) + _mm(attns[x], v_news[x])
    for x, (bb, h) in enumerate(units):
        kd = ks[x] * col(er_alls, bb, GDN_HEADS + h)
        s_ref[bb * GDN_HEADS + h] = (ss[x] * col(el_alls, bb, GDN_HEADS + h)
                                     + _mm_tn(kd, v_news[x]))

    @pl.when(n == pl.num_programs(2) - 1)
    def _():
        for bb in range(nb):
            sout_ref[0, bb] = s_ref[bb * GDN_HEADS:(bb + 1) * GDN_HEADS]


def gdn_chunks(qkv, p, ba_tile0, a_rows, dt_rows, s0, *, nb):
    b, l, _ = qkv.shape
    nc = l // GDN_CHUNK
    nb = math.gcd(nb, b)

    def row(d, bi, n):
        return n + d * (nc - 1 - 2 * n)

    return pl.pallas_call(
        _gdn_chunk_kernel,
        grid=(2, b // nb, nc),
        in_specs=[
            pl.BlockSpec((nb, GDN_CHUNK, GDN_W), lambda d, bi, n: (bi, row(d, bi, n), 0)),
            pl.BlockSpec((nb, GDN_CHUNK, GDN_W), lambda d, bi, n: (bi, row(d, bi, n), 1)),
            pl.BlockSpec((nb, GDN_CHUNK, GDN_W), lambda d, bi, n: (bi, row(d, bi, n), 2)),
            pl.BlockSpec((nb, GDN_CHUNK, LANES), lambda d, bi, n: (bi, row(d, bi, n), ba_tile0 + d)),
            pl.BlockSpec((1, 1, LANES), lambda d, bi, n: (d, 0, 0)),
            pl.BlockSpec((1, 1, LANES), lambda d, bi, n: (d, 0, 0)),
            pl.BlockSpec((1, nb, GDN_HEADS, GDN_DK, GDN_DV), lambda d, bi, n: (d, bi, 0, 0, 0)),
        ],
        out_specs=[
            pl.BlockSpec((1, nb, GDN_CHUNK, GDN_W), lambda d, bi, n: (d, bi, row(d, bi, n), 0)),
            pl.BlockSpec((1, nb, GDN_HEADS, GDN_DK, GDN_DV), lambda d, bi, n: (d, bi, 0, 0, 0)),
        ],
        out_shape=[
            jax.ShapeDtypeStruct((2, b, l, GDN_W), jnp.float32),
            jax.ShapeDtypeStruct((2, b, GDN_HEADS, GDN_DK, GDN_DV), jnp.float32),
        ],
        scratch_shapes=[pltpu.VMEM((nb * GDN_HEADS, GDN_DK, GDN_DV), jnp.float32)],
        compiler_params=_cparams(("parallel", "parallel", "arbitrary"), 40),
        name="gdn_chunks",
    )(qkv, qkv, qkv, p, a_rows, dt_rows, s0)


def _gated_out_kernel(of_ref, ob_ref, z_ref, w_ref, y_ref):
    for h in range(GDN_HEADS):
        sl = slice(h * GDN_DV, (h + 1) * GDN_DV)
        o = of_ref[0, 0, :, sl] + ob_ref[0, 0, :, sl]
        y = o * lax.rsqrt(jnp.mean(o * o, axis=-1, keepdims=True) + EPS)
        y_ref[0, :, sl] = (y * w_ref[...] * _silu(z_ref[0, :, sl])).astype(y_ref.dtype)


def gated_out(o, p, z_tile, w_norm, *, tm):
    _, b, l, _ = o.shape
    tm = min(tm, l)
    return pl.pallas_call(
        _gated_out_kernel,
        grid=(b, l // tm),
        in_specs=[
            pl.BlockSpec((1, 1, tm, GDN_W), lambda bi, i: (0, bi, i, 0)),
            pl.BlockSpec((1, 1, tm, GDN_W), lambda bi, i: (1, bi, i, 0)),
            pl.BlockSpec((1, tm, GDN_W), lambda bi, i: (bi, i, z_tile)),
            pl.BlockSpec((1, GDN_DV), lambda bi, i: (0, 0)),
        ],
        out_specs=pl.BlockSpec((1, tm, GDN_W), lambda bi, i: (bi, i, 0)),
        out_shape=jax.ShapeDtypeStruct((b, l, GDN_W), MXU_DTYPE),
        compiler_params=_cparams(("parallel", "parallel"), 40),
        name="gated_out",
    )(o, o, p, w_norm.reshape(1, GDN_DV))


def _fourier_kernel(x_ref, bd_ref, cs_ref, o_ref, z_ref):
    i = pl.program_id(1)
    l = x_ref.shape[1]

    @pl.when(i == 0)
    def _():
        z = _mm(x_ref[0], bd_ref[...])
        z_ref[0:l, :] = z[:, :FNET_W].astype(z_ref.dtype)
        z_ref[l:2 * l, :] = z[:, FNET_W:].astype(z_ref.dtype)

    o_ref[0] = jnp.dot(cs_ref[...], z_ref[...],
                       preferred_element_type=jnp.float32).astype(o_ref.dtype)


def _dft_tables(l):
    def cs(n, scale):
        k = jnp.arange(n, dtype=jnp.int32)
        ang = ((k[:, None] * k[None, :]) % n).astype(jnp.float32) * (2.0 * math.pi / n)
        return jnp.cos(ang) * scale, jnp.sin(ang) * scale

    c_l, s_l = cs(l, 1.0)
    c_g, s_g = cs(FNET_GW, 1.0 / math.sqrt(l * FNET_GW))
    eye = jnp.eye(FNET_GROUPS, dtype=jnp.float32)
    bd = jnp.concatenate([jnp.kron(eye, c_g), jnp.kron(eye, s_g)], axis=1)
    cs_l = jnp.concatenate([c_l, -s_l], axis=1)
    return bd.astype(MXU_DTYPE), cs_l.astype(MXU_DTYPE)


def fourier_mix(p, f_tile, *, tm):
    b, l, _ = p.shape
    tm = min(tm, l)
    bd, cs_l = _dft_tables(l)
    return pl.pallas_call(
        _fourier_kernel,
        grid=(b, l // tm),
        in_specs=[
            pl.BlockSpec((1, l, FNET_W), lambda bi, i: (bi, 0, f_tile)),
            pl.BlockSpec((FNET_W, 2 * FNET_W), lambda bi, i: (0, 0)),
            pl.BlockSpec((tm, 2 * l), lambda bi, i: (i, 0)),
        ],
        out_specs=pl.BlockSpec((1, tm, FNET_W), lambda bi, i: (bi, i, 0)),
        out_shape=jax.ShapeDtypeStruct((b, l, FNET_W), MXU_DTYPE),
        scratch_shapes=[pltpu.VMEM((2 * l, FNET_W), MXU_DTYPE)],
        compiler_params=_cparams(("parallel", "arbitrary"), 40),
        name="fourier_mix",
    )(p, bd, cs_l)


def _swa_kernel(*refs, use_window):
    if use_window:
        sink_ref, q_ref, k_ref, v_ref, kc_ref, vc_ref, o_ref = refs
    else:
        sink_ref, q_ref, kc_ref, vc_ref, o_ref = refs
    i = pl.program_id(1)
    tq = q_ref.shape[1]
    scale = SWA_DH ** -0.5
    if use_window:
        l = k_ref.shape[1]
        span = SWA_BLOCK + 2 * SWA_WINDOW
        start = i * tq
        base = pl.multiple_of(jnp.clip(start - SWA_WINDOW, 0, l - span), SWA_BLOCK)
        qpos = start + lax.broadcasted_iota(jnp.int32, (tq, span), 0)
        kpos = base + lax.broadcasted_iota(jnp.int32, (tq, span), 1)
        band = jnp.abs(qpos - kpos) <= SWA_WINDOW
    for g in range(SWA_KV):
        ksl = slice(g * SWA_DH, (g + 1) * SWA_DH)
        qg = jnp.concatenate(
            [q_ref[0, :, (g * SWA_GROUP + hh) * SWA_DH:(g * SWA_GROUP + hh + 1) * SWA_DH]
             for hh in range(SWA_GROUP)], axis=0)
        kc = kc_ref[0, :, ksl]
        vc = vc_ref[0, :, ksl]
        lc_all = _mm_nt(qg, kc) * scale
        if use_window:
            kw = k_ref[0, pl.ds(base, span), ksl]
            vw = v_ref[0, pl.ds(base, span), ksl]
            lw_all = _mm_nt(qg, kw) * scale
        outs = []
        for hh in range(SWA_GROUP):
            rs = slice(hh * tq, (hh + 1) * tq)
            sink = sink_ref[g * SWA_GROUP + hh]
            lc = lc_all[rs]
            m = jnp.maximum(jnp.max(lc, axis=-1, keepdims=True), sink)
            if use_window:
                lw = jnp.where(band, lw_all[rs], -jnp.inf)
                m = jnp.maximum(m, jnp.max(lw, axis=-1, keepdims=True))
                pw = jnp.exp(lw - m)
            pc = jnp.exp(lc - m)
            den = jnp.sum(pc, axis=-1, keepdims=True) + jnp.exp(sink - m)
            acc = _mm(pc, vc)
            if use_window:
                den = den + jnp.sum(pw, axis=-1, keepdims=True)
                acc = acc + _mm(pw, vw)
            outs.append(acc / den)
        for pair in range(SWA_GROUP // 2):
            lo = (g * SWA_GROUP + 2 * pair) * SWA_DH
            o_ref[0, :, lo:lo + 2 * SWA_DH] = jnp.concatenate(
                [outs[2 * pair], outs[2 * pair + 1]], axis=1).astype(o_ref.dtype)


def swa_attention(qkv, ckv, sinks, *, use_window):
    b, l, _ = qkv.shape
    c = ckv.shape[1]
    tq = SWA_BLOCK
    kt = SWA_Q_W // SWA_KV_W
    in_specs = [
        pl.BlockSpec(memory_space=pltpu.SMEM),
        pl.BlockSpec((1, tq, SWA_Q_W), lambda bi, i: (bi, i, 0)),
    ]
    args = [sinks, qkv]
    if use_window:
        in_specs += [
            pl.BlockSpec((1, l, SWA_KV_W), lambda bi, i: (bi, 0, kt)),
            pl.BlockSpec((1, l, SWA_KV_W), lambda bi, i: (bi, 0, kt + 1)),
        ]
        args += [qkv, qkv]
    in_specs += [
        pl.BlockSpec((1, c, SWA_KV_W), lambda bi, i: (bi, 0, 0)),
        pl.BlockSpec((1, c, SWA_KV_W), lambda bi, i: (bi, 0, 1)),
    ]
    args += [ckv, ckv]
    return pl.pallas_call(
        functools.partial(_swa_kernel, use_window=use_window),
        grid=(b, l // tq),
        in_specs=in_specs,
        out_specs=pl.BlockSpec((1, tq, SWA_Q_W), lambda bi, i: (bi, i, 0)),
        out_shape=jax.ShapeDtypeStruct((b, l, SWA_Q_W), MXU_DTYPE),
        compiler_params=_cparams(("parallel", "parallel"), 40),
        name="swa_attention",
    )(*args)


def _rope_tables(l):
    rows = l // GRID_W
    r = jnp.broadcast_to(jnp.arange(rows)[:, None], (rows, GRID_W)).reshape(-1).astype(jnp.float32)
    col = jnp.broadcast_to(jnp.arange(GRID_W)[None, :], (rows, GRID_W)).reshape(-1).astype(jnp.float32)
    n = SWA_DH // 4
    freq = ROPE_BASE ** (-jnp.arange(n, dtype=jnp.float32) / n)
    ang_r = r[:, None] * freq
    ang_c = col[:, None] * freq
    cos = jnp.concatenate([jnp.cos(ang_r)] * 2 + [jnp.cos(ang_c)] * 2, axis=-1)
    sin = jnp.concatenate([-jnp.sin(ang_r), jnp.sin(ang_r), -jnp.sin(ang_c), jnp.sin(ang_c)], axis=-1)
    reps = LANES // SWA_DH
    return jnp.tile(cos, (1, reps)), jnp.tile(sin, (1, reps))


def _sort_desc(rows):
    n = len(rows)
    assert n & (n - 1) == 0
    rows = list(rows)
    k = 2
    while k <= n:
        j = k // 2
        while j >= 1:
            for i in range(n):
                p = i ^ j
                if p > i:
                    hi, lo = jnp.maximum(rows[i], rows[p]), jnp.minimum(rows[i], rows[p])
                    rows[i], rows[p] = (hi, lo) if (i & k) == 0 else (lo, hi)
            j //= 2
        k *= 2
    return rows


def _top_values(s, count):
    levels = _sort_desc([s[SUBLANES * l:SUBLANES * (l + 1)] for l in range(s.shape[0] // SUBLANES)])
    levels = levels[:count]
    vals = []
    for k in range(count):
        m = jnp.max(levels[0], axis=0, keepdims=True)
        vals.append(m)
        hit = levels[0] == m
        keep = max(count - k - 1, 1)
        below = levels[1:] + [jnp.full_like(levels[0], -jnp.inf)]
        levels = [jnp.where(hit, b, a) for a, b in zip(levels[:keep], below[:keep])]
    return vals


def _peer_cells():
    n = PEER_TOPK + 1
    return [(a, b) for a in range(n) for b in range(n) if (a + 1) * (b + 1) <= n]


def _peer_topk_kernel(q_ref, keys_ref, s1_ref, e1_ref, d_ref, e0_ref):
    n = PEER_TOPK + 1
    cells = _peer_cells()
    for h in range(PEER_HEADS):
        q0 = q_ref[:, (2 * h) * PEER_HALF:(2 * h + 1) * PEER_HALF]
        q1 = q_ref[:, (2 * h + 1) * PEER_HALF:(2 * h + 2) * PEER_HALF]
        s0 = lax.dot_general(keys_ref[0], q0, _NT, precision=_HI,
                             preferred_element_type=jnp.float32)
        s1 = lax.dot_general(keys_ref[1], q1, _NT, precision=_HI,
                             preferred_element_type=jnp.float32)
        top0 = _top_values(s0, n)
        top1 = _top_values(s1, n)
        rows = [top0[a] + top1[b] for a, b in cells]
        rows += [jnp.full_like(rows[0], -jnp.inf)] * (-len(rows) % (8 * SUBLANES))
        best = _top_values(jnp.concatenate(rows, axis=0), n)
        thr = 0.5 * (best[PEER_TOPK - 1] + best[PEER_TOPK])
        z = jnp.ones_like(thr)
        for kk in range(1, PEER_TOPK):
            z = z + jnp.exp(best[kk] - best[0])
        s1_ref[h] = s1
        e1_ref[h] = jnp.exp(s1 - top1[0]) / z
        d_ref[h] = thr - s0
        e0_ref[h] = jnp.exp(s0 - top0[0])


def peer_topk(q, keys, *, tt):
    t = q.shape[0]
    tt = min(tt, t)
    out = jax.ShapeDtypeStruct((PEER_HEADS, PEER_NKEYS, t), jnp.float32)
    ospec = pl.BlockSpec((PEER_HEADS, PEER_NKEYS, tt), lambda i: (0, 0, i))
    return pl.pallas_call(
        _peer_topk_kernel,
        grid=(t // tt,),
        in_specs=[
            pl.BlockSpec((tt, q.shape[1]), lambda i: (i, 0)),
            pl.BlockSpec((2, PEER_NKEYS, PEER_HALF), lambda i: (0, 0, 0)),
        ],
        out_specs=[ospec] * 4,
        out_shape=[out] * 4,
        compiler_params=_cparams(("parallel",), 40),
        name="peer_topk",
    )(q, keys)


def _gelu(x):
    return 0.5 * x * (1.0 + lax.erf(x * (2.0 ** -0.5)))


def _peer_dense_kernel(*refs, final_norm):
    (hmt_ref, u0_ref, ua_ref, ub_ref, vt_ref, s1_ref, e1_ref, d_ref, e0_ref, x_ref,
     gate_ref) = refs[:11]
    pos = 11
    if final_norm:
        fn_ref = refs[pos]
        pos += 1
    o_ref, acc_ref, pt_ref, at_ref = refs[pos:pos + 4]
    e = pl.program_id(1)
    eg, tm = at_ref.shape
    ni = eg // PEER_NKEYS

    def first_matmul(u_ref):
        return jnp.dot(u_ref[...], hmt_ref[...], preferred_element_type=jnp.float32)

    @pl.when(e == 0)
    def _():
        acc_ref[...] = jnp.zeros_like(acc_ref)
        at_ref[...] = first_matmul(u0_ref)

    def gate_group(r, a_block):
        for il in range(ni):
            ig = r * ni + il
            rs = slice(ig * PEER_NKEYS, (ig + 1) * PEER_NKEYS)
            for tl in range(tm // LANES):
                cs = slice(tl * LANES, (tl + 1) * LANES)
                g = None
                for h in range(PEER_HEADS):
                    drow = d_ref[h, ig:ig + 1, cs]
                    erow = e0_ref[h, ig:ig + 1, cs]
                    term = jnp.where(s1_ref[h, :, cs] >= drow, e1_ref[h, :, cs] * erow, 0.0)
                    g = term if g is None else g + term
                pt_ref[rs, cs] = (g * _gelu(a_block(il, cs))).astype(pt_ref.dtype)

    def second_matmul(r):
        es = slice(r * eg, (r + 1) * eg)
        acc_ref[...] += jnp.dot(vt_ref[:, es], pt_ref[es, :], preferred_element_type=jnp.float32)

    a1 = first_matmul(ua_ref)
    gate_group(0, lambda il, cs: at_ref[il * PEER_NKEYS:(il + 1) * PEER_NKEYS, cs])
    second_matmul(0)
    at_ref[...] = first_matmul(ub_ref)
    gate_group(1, lambda il, cs: a1[il * PEER_NKEYS:(il + 1) * PEER_NKEYS, cs])
    second_matmul(1)

    @pl.when(e == pl.num_programs(1) - 1)
    def _():
        y = x_ref[...] + gate_ref[0] * acc_ref[...].T
        if final_norm:
            y = y * lax.rsqrt(jnp.mean(y * y, axis=-1, keepdims=True) + EPS) * fn_ref[...]
        o_ref[...] = y


def peer_dense(hmt, u, vt, layer, sel, x, gate, rows_per_gate, final_gain=None, *, tm, eg):
    t, d = x.shape
    ne = u.shape[1]
    tm = min(tm, t, rows_per_gate)
    te = 2 * eg
    assert vt.shape[1:] == (ne // te, d, te)
    assert rows_per_gate % tm == 0 and t % tm == 0 and ne % te == 0 and eg % PEER_NKEYS == 0
    ni = te // PEER_NKEYS
    assert ni % SUBLANES == 0
    s1, e1, dd, e0 = sel
    gpb = rows_per_gate // tm
    last = ne // eg - 1
    once = pl.Buffered(1)
    full = pl.BlockSpec((PEER_HEADS, PEER_NKEYS, tm), lambda i, e: (0, 0, i))
    rowsp = pl.BlockSpec((PEER_HEADS, ni, tm), lambda i, e: (0, e, i))
    in_specs = [
        pl.BlockSpec((d, tm), lambda i, e: (0, i)),
        pl.BlockSpec((None, eg, d), lambda i, e: (layer, 0, 0), pipeline_mode=once),
        pl.BlockSpec((None, eg, d), lambda i, e: (layer, 2 * e + 1, 0)),
        pl.BlockSpec((None, eg, d), lambda i, e: (layer, jnp.minimum(2 * e + 2, last), 0)),
        pl.BlockSpec((None, None, d, te), lambda i, e: (layer, e, 0, 0)),
        full, full, rowsp, rowsp,
        pl.BlockSpec((tm, d), lambda i, e: (i, 0), pipeline_mode=once),
        pl.BlockSpec((1, 1, d), lambda i, e: (i // gpb, 0, 0)),
    ]
    args = [hmt, u, u, u, vt, s1, e1, dd, e0, x, gate.reshape(gate.shape[0], 1, d)]
    if final_gain is not None:
        in_specs.append(pl.BlockSpec((1, d), lambda i, e: (0, 0)))
        args.append(final_gain.reshape(1, d))
    return pl.pallas_call(
        functools.partial(_peer_dense_kernel, final_norm=final_gain is not None),
        grid=(t // tm, ne // te),
        in_specs=in_specs,
        out_specs=pl.BlockSpec((tm, d), lambda i, e: (i, 0)),
        out_shape=jax.ShapeDtypeStruct((t, d), jnp.float32),
        scratch_shapes=[pltpu.VMEM((d, tm), jnp.float32), pltpu.VMEM((te, tm), MXU_DTYPE),
                        pltpu.VMEM((eg, tm), jnp.float32)],
        compiler_params=_cparams(("arbitrary", "arbitrary"), 60),
        name="peer_dense",
    )(*args)


def peer_block(x, gain, shift, scale, gate, w_q, keys, u, vt, layer, final_gain=None):
    b, l, d = x.shape
    q, hmt = mod_matmul(x, gain, shift, scale, w_q, tm=1024, tn=512, emit_a=True)
    sel = peer_topk(q.reshape(b * l, d), keys, tt=256)
    out = peer_dense(hmt, u, vt, layer, sel, x.reshape(b * l, d), gate, l,
                     final_gain, tm=512, eg=PEER_EXPERT_GROUP)
    return out.reshape(b, l, d)


def _pack_even_w_in(w_in):
    d = w_in.shape[0]
    o = 4 * GDN_W
    h = GDN_HEADS
    ba = w_in[:, o:o + 4 * h]
    zeros = jnp.zeros((d, LANES - 2 * h), w_in.dtype)
    fwd = jnp.concatenate([ba[:, 0:h], ba[:, 2 * h:3 * h], zeros], axis=1)
    bwd = jnp.concatenate([ba[:, h:2 * h], ba[:, 3 * h:4 * h], zeros], axis=1)
    pad = jnp.zeros((d, 2 * LANES), w_in.dtype)
    packed = jnp.concatenate([w_in[:, :o], w_in[:, o + 4 * h:], fwd, bwd, pad], axis=1)
    return packed.astype(MXU_DTYPE)


def _lane_rows(vals):
    z = jnp.zeros((2, GDN_HEADS), jnp.float32)
    pad = jnp.zeros((2, LANES - 2 * GDN_HEADS), jnp.float32)
    return jnp.concatenate([z, vals.astype(jnp.float32), pad], axis=1).reshape(2, 1, LANES)


def even_layer_mixer(x, h_c, m_lat, m_ctx, gain, w_in, conv_w, a_log, dt_bias, out_norm, w_out,
                     ctx_out):
    b = x.shape[0]
    w_packed = _pack_even_w_in(w_in)
    ba_tile0 = (4 * GDN_W + FNET_W) // LANES
    z_tile = 3
    f_tile = 4 * GDN_W // FNET_W
    a_rows = _lane_rows(jnp.exp(a_log.astype(jnp.float32)))
    dt_rows = _lane_rows(dt_bias)
    w_out_c = w_out.astype(MXU_DTYPE)
    w_y, w_f = w_out_c[:GDN_W], w_out_c[GDN_W:]

    p_ctx = mod_matmul(h_c, gain, m_ctx[0], m_ctx[1], w_packed, tm=512, tn=512)
    p_lat = mod_matmul(x, gain, m_lat[0], m_lat[1], w_packed, tm=1024, tn=512)
    zero = jnp.zeros((2, b, GDN_HEADS, GDN_DK, GDN_DV), jnp.float32)
    o_ctx, s_ctx = gdn_chunks(gdn_short_conv(p_ctx, conv_w), p_ctx, ba_tile0, a_rows, dt_rows, zero,
                              nb=2)
    o_lat, _ = gdn_chunks(gdn_short_conv(p_lat, conv_w), p_lat, ba_tile0, a_rows, dt_rows, s_ctx,
                          nb=2)

    def finish(p, o, resid, gate):
        y = gated_out(o, p, z_tile, out_norm, tm=512)
        f = fourier_mix(p, f_tile, tm=512)
        return resid_matmul([y, f], [w_y, w_f], resid, gate, tm=1024, tn=512)

    x = finish(p_lat, o_lat, x, m_lat[2])
    if ctx_out:
        h_c = finish(p_ctx, o_ctx, h_c, m_ctx[2])
    return x, h_c


def odd_layer_mixer(x, h_c, m_lat, m_ctx, gain, w_qkv, sinks, w_out, ctx_out):
    l = x.shape[1]
    w_c = w_qkv.astype(MXU_DTYPE)
    w_out_c = w_out.astype(MXU_DTYPE)
    rope = _rope_tables(l)
    tn = 256
    qkv = mod_matmul(x, gain, m_lat[0], m_lat[1], w_c, tm=1024, tn=tn,
                     rope=rope, rope_tiles=(SWA_Q_W + SWA_KV_W) // tn)
    ckv = mod_matmul(h_c, gain, m_ctx[0], m_ctx[1], w_c[:, SWA_Q_W:], tm=512, tn=tn)
    sinks = sinks.astype(jnp.float32)
    o = swa_attention(qkv, ckv, sinks, use_window=True)
    x_new = resid_matmul([o], [w_out_c], x, m_lat[2], tm=1024, tn=512)
    if ctx_out:
        q_c = mod_matmul(h_c, gain, m_ctx[0], m_ctx[1], w_c[:, :SWA_Q_W], tm=512, tn=tn)
        o_c = swa_attention(q_c, ckv, sinks, use_window=False)
        h_c = resid_matmul([o_c], [w_out_c], h_c, m_ctx[2], tm=1024, tn=512)
    return x_new, h_c


def kernel(x, c, ctx, c_ctx, ada_w, ada_b, norm_mix, norm_ffn, even_w_in, gdn_conv, gdn_a_log, gdn_dt_bias, gdn_out_norm, even_w_out, odd_w_qkv, odd_sinks, odd_w_out, peer_w_q, peer_keys, peer_u, peer_v, final_norm):
    b, _, d = x.shape
    depth = ada_w.shape[0]
    rows = 16
    cond = jnp.concatenate([c, c_ctx[None, :], jnp.zeros((rows - b - 1, d), c.dtype)], axis=0)
    mods = ada_mod_all(cond, ada_w, ada_b)
    conv_w_all = gdn_conv
    u_all = peer_u.astype(MXU_DTYPE)
    te = 2 * PEER_EXPERT_GROUP
    vt_all = peer_v.astype(MXU_DTYPE).reshape(depth, -1, te, d).transpose(0, 1, 3, 2)
    h_c = ctx
    for i in range(depth):
        last = i == depth - 1
        j = i // 2
        m = mods[i].reshape(rows, N_MOD, d)
        m_lat = [m[:b, k] for k in range(N_MOD)]
        m_ctx = [jnp.broadcast_to(m[b, k][None, :], (b, d)) for k in range(N_MOD)]
        if i % 2 == 0:
            x, h_c = even_layer_mixer(x, h_c, m_lat, m_ctx, norm_mix[i], even_w_in[j], conv_w_all[j],
                                      gdn_a_log[j], gdn_dt_bias[j], gdn_out_norm[j], even_w_out[j],
                                      not last)
        else:
            x, h_c = odd_layer_mixer(x, h_c, m_lat, m_ctx, norm_mix[i], odd_w_qkv[j], odd_sinks[j],
                                     odd_w_out[j], not last)
        w_q = peer_w_q[i].astype(MXU_DTYPE)
        keys = peer_keys[i].astype(jnp.float32)
        x = peer_block(x, norm_ffn[i], m_lat[3], m_lat[4], m_lat[5], w_q, keys, u_all, vt_all, i,
                       final_norm if last else None)
        if not last:
            h_c = peer_block(h_c, norm_ffn[i], m_ctx[3], m_ctx[4], m_ctx[5], w_q, keys,
                             u_all, vt_all, i)
    return x
```

```python
import functools
import math

import jax
import jax.numpy as jnp
from jax import lax
from jax.experimental import pallas as pl
from jax.experimental.pallas import tpu as pltpu

N_MOD = 6
EPS = 1e-6
GDN_HEADS = 12
GDN_DK = 128
GDN_DV = 128
GDN_W = GDN_HEADS * GDN_DV
GDN_CONV = 5
GDN_CHUNK = 64
GDN_PACK = 2
FNET_GROUPS = 4
FNET_GW = 128
FNET_W = FNET_GROUPS * FNET_GW
GRID_W = 64
SWA_HEADS = 32
SWA_KV = 4
SWA_GROUP = SWA_HEADS // SWA_KV
SWA_DH = 64
SWA_WINDOW = 128
SWA_BLOCK = 128
ROPE_BASE = 10000.0
SWA_Q_W = SWA_HEADS * SWA_DH
SWA_KV_W = SWA_KV * SWA_DH
PEER_HEADS = 8
PEER_NKEYS = 128
PEER_HALF = 128
PEER_TOPK = 16
PEER_EXPERT_GROUP = 512

LANES = 128
SUBLANES = 8
VMEM_BYTES_V7X = 64 * 1024 * 1024
MXU_DTYPE = jnp.bfloat16

_HI = lax.Precision.HIGHEST
_NT = (((1,), (1,)), ((), ()))
_TN = (((0,), (0,)), ((), ()))


def _cparams(semantics, vmem_mb):
    assert vmem_mb * 1024 * 1024 < VMEM_BYTES_V7X
    return pltpu.CompilerParams(dimension_semantics=semantics,
                                vmem_limit_bytes=vmem_mb * 1024 * 1024)


def _mm(a, b):
    return jnp.dot(a.astype(MXU_DTYPE), b.astype(MXU_DTYPE),
                   preferred_element_type=jnp.float32)


def _mm_nt(a, b):
    return lax.dot_general(a.astype(MXU_DTYPE), b.astype(MXU_DTYPE), _NT,
                           preferred_element_type=jnp.float32)


def _mm_tn(a, b):
    return lax.dot_general(a.astype(MXU_DTYPE), b.astype(MXU_DTYPE), _TN,
                           preferred_element_type=jnp.float32)


def _mm_f32(a, b):
    return jnp.dot(a, b, precision=_HI, preferred_element_type=jnp.float32)


def _split(a):
    hi = a.astype(MXU_DTYPE)
    lo = (a - hi.astype(jnp.float32)).astype(MXU_DTYPE)
    return hi, lo


def _mm_x3(a, b):
    (ah, al), (bh, bl) = a, b
    dot = functools.partial(jnp.dot, preferred_element_type=jnp.float32)
    m = ah.shape[0]
    both = dot(jnp.concatenate([ah, al], axis=0), bh)
    return both[:m] + (dot(ah, bl) + both[m:])


def _sigmoid(x):
    return 1.0 / (1.0 + jnp.exp(-x))


def _silu(x):
    return x * _sigmoid(x)


def _softplus(x):
    return jnp.maximum(x, 0.0) + jnp.log1p(jnp.exp(-jnp.abs(x)))


def _ada_kernel(cond_ref, w_ref, b_ref, o_ref):
    a = _silu(cond_ref[...])
    o_ref[0] = _mm(a, w_ref[0]) + b_ref[0]


def ada_mod_all(cond, ada_w, ada_b, *, tn=1024):
    depth, d, n = ada_w.shape
    r = cond.shape[0]
    return pl.pallas_call(
        _ada_kernel,
        grid=(depth, n // tn),
        in_specs=[
            pl.BlockSpec((r, d), lambda l, j: (0, 0)),
            pl.BlockSpec((1, d, tn), lambda l, j: (l, 0, j)),
            pl.BlockSpec((1, 1, tn), lambda l, j: (l, 0, j)),
        ],
        out_specs=pl.BlockSpec((1, r, tn), lambda l, j: (l, 0, j)),
        out_shape=jax.ShapeDtypeStruct((depth, r, n), jnp.float32),
        compiler_params=_cparams(("parallel", "parallel"), 40),
        name="ada_mod",
    )(cond, ada_w, ada_b.reshape(depth, 1, n))


def _rope_swap(y):
    n = y.shape[-1]
    lane = lax.broadcasted_iota(jnp.int32, y.shape, 1)
    up = pltpu.roll(y, n - 16, 1)
    down = pltpu.roll(y, 16, 1)
    return jnp.where(lane % 32 < 16, up, down)


def _mod_matmul_kernel(*refs, rope_tiles, emit_a, tn):
    x_ref, gain_ref, shift_ref, scale_ref, w_ref = refs[:5]
    pos = 5
    if rope_tiles:
        cos_ref, sin_ref = refs[pos:pos + 2]
        pos += 2
    o_ref = refs[pos]
    pos += 1
    if emit_a:
        a_out_ref = refs[pos]
        pos += 1
    a_ref = refs[pos]
    j = pl.program_id(2)

    @pl.when(j == 0)
    def _():
        x = x_ref[0]
        y = x * lax.rsqrt(jnp.mean(x * x, axis=-1, keepdims=True) + EPS)
        y = y * gain_ref[...]
        a = y * (1.0 + scale_ref[0]) + shift_ref[0]
        a_ref[...] = a.astype(a_ref.dtype)
        if emit_a:
            a_out_ref[...] = a.T.astype(a_out_ref.dtype)

    acc = jnp.dot(a_ref[...], w_ref[...], preferred_element_type=jnp.float32)
    if rope_tiles:
        @pl.when(j < rope_tiles)
        def _():
            cos = cos_ref[...]
            sin = sin_ref[...]
            for t in range(tn // LANES):
                sl = slice(t * LANES, (t + 1) * LANES)
                y = acc[:, sl]
                o_ref[0, :, sl] = (y * cos + _rope_swap(y) * sin).astype(o_ref.dtype)

        @pl.when(j >= rope_tiles)
        def _():
            o_ref[0] = acc.astype(o_ref.dtype)
    else:
        o_ref[0] = acc.astype(o_ref.dtype)


def mod_matmul(x, gain, shift, scale, w, *, tm, tn, rope=None, rope_tiles=0, emit_a=False):
    b, l, d = x.shape
    n = w.shape[1]
    tm = min(tm, l)
    assert l % tm == 0 and n % tn == 0 and tn % LANES == 0
    in_specs = [
        pl.BlockSpec((1, tm, d), lambda bi, i, j: (bi, i, 0)),
        pl.BlockSpec((1, d), lambda bi, i, j: (0, 0)),
        pl.BlockSpec((1, 1, d), lambda bi, i, j: (bi, 0, 0)),
        pl.BlockSpec((1, 1, d), lambda bi, i, j: (bi, 0, 0)),
        pl.BlockSpec((d, tn), lambda bi, i, j: (0, j)),
    ]
    args = [x, gain.reshape(1, d), shift.reshape(b, 1, d), scale.reshape(b, 1, d), w]
    if rope_tiles:
        in_specs += [pl.BlockSpec((tm, LANES), lambda bi, i, j: (i, 0))] * 2
        args += list(rope)
    out_specs = [pl.BlockSpec((1, tm, tn), lambda bi, i, j: (bi, i, j))]
    out_shape = [jax.ShapeDtypeStruct((b, l, n), jnp.float32)]
    if emit_a:
        nb = l // tm
        out_specs.append(pl.BlockSpec((d, tm), lambda bi, i, j: (0, bi * nb + i)))
        out_shape.append(jax.ShapeDtypeStruct((d, b * l), w.dtype))
    res = pl.pallas_call(
        functools.partial(_mod_matmul_kernel, rope_tiles=rope_tiles, emit_a=emit_a, tn=tn),
        grid=(b, l // tm, n // tn),
        in_specs=in_specs,
        out_specs=out_specs,
        out_shape=out_shape,
        scratch_shapes=[pltpu.VMEM((tm, d), w.dtype)],
        compiler_params=_cparams(("parallel", "parallel", "arbitrary"), 56),
        name="mod_matmul",
    )(*args)
    return res if emit_a else res[0]


def _resid_matmul_kernel(*refs, n_pairs):
    a_refs = refs[:n_pairs]
    w_refs = refs[n_pairs:2 * n_pairs]
    x_ref, gate_ref, o_ref = refs[2 * n_pairs:]
    acc = jnp.dot(a_refs[0][0], w_refs[0][...], preferred_element_type=jnp.float32)
    for a_ref, w_ref in zip(a_refs[1:], w_refs[1:]):
        acc += jnp.dot(a_ref[0], w_ref[...], preferred_element_type=jnp.float32)
    o_ref[0] = x_ref[0] + gate_ref[0] * acc


def resid_matmul(acts, weights, x, gate, *, tm, tn):
    b, l, n = x.shape
    tm = min(tm, l)
    in_specs = []
    for a in acts:
        in_specs.append(pl.BlockSpec((1, tm, a.shape[2]), lambda bi, i, j: (bi, i, 0)))
    for w in weights:
        in_specs.append(pl.BlockSpec((w.shape[0], tn), lambda bi, i, j: (0, j)))
    in_specs += [
        pl.BlockSpec((1, tm, tn), lambda bi, i, j: (bi, i, j)),
        pl.BlockSpec((1, 1, tn), lambda bi, i, j: (bi, 0, j)),
    ]
    return pl.pallas_call(
        functools.partial(_resid_matmul_kernel, n_pairs=len(acts)),
        grid=(b, l // tm, n // tn),
        in_specs=in_specs,
        out_specs=pl.BlockSpec((1, tm, tn), lambda bi, i, j: (bi, i, j)),
        out_shape=jax.ShapeDtypeStruct((b, l, n), jnp.float32),
        compiler_params=_cparams(("parallel", "parallel", "parallel"), 40),
        name="resid_matmul",
    )(*acts, *weights, x, gate.reshape(b, 1, n))


def _gdn_conv_kernel(p_ref, w_ref, o_ref):
    c = pl.program_id(1)
    x = p_ref[0]
    l = x.shape[0]
    row = lax.broadcasted_iota(jnp.int32, x.shape, 0)
    pad = (GDN_CONV - 1) // 2
    y = x * w_ref[0, pad:pad + 1, :]
    for t in range(GDN_CONV):
        s = t - pad
        if s == 0:
            continue
        xs = pltpu.roll(x, (-s) % l, 0)
        ok = (row + s >= 0) & (row + s < l)
        y = y + jnp.where(ok, xs, 0.0) * w_ref[0, t:t + 1, :]
    y = _silu(y)
    inv = lax.rsqrt(jnp.sum(y * y, axis=-1, keepdims=True) + 1e-6)
    fac = jnp.where(c < 2 * GDN_HEADS, inv, 1.0)
    fac = fac * jnp.where(c < GDN_HEADS, GDN_DK ** -0.5, 1.0)
    o_ref[0] = y * fac


def gdn_short_conv(p, conv_w):
    b, l, _ = p.shape
    nt = 3 * GDN_W // LANES
    w = conv_w.reshape(GDN_CONV, nt, LANES).transpose(1, 0, 2)
    return pl.pallas_call(
        _gdn_conv_kernel,
        grid=(b, nt),
        in_specs=[
            pl.BlockSpec((1, l, LANES), lambda bi, c: (bi, 0, c)),
            pl.BlockSpec((1, GDN_CONV, LANES), lambda bi, c: (c, 0, 0)),
        ],
        out_specs=pl.BlockSpec((1, l, LANES), lambda bi, c: (bi, 0, c)),
        out_shape=jax.ShapeDtypeStruct((b, l, 3 * GDN_W), jnp.float32),
        compiler_params=_cparams(("parallel", "parallel"), 32),
        name="gdn_conv",
    )(p, w)


def _lane_blocks(cat, n):
    shift = int(math.log2(cat.shape[1] // n))
    blk = jnp.right_shift(lax.broadcasted_iota(jnp.int32, cat.shape, 1), shift)
    return [jnp.where(blk == b, cat, 0.0) for b in range(n)]


def _block_diag_split(cat, n):
    hi = cat.astype(MXU_DTYPE).astype(jnp.float32)
    lo = cat - hi
    return tuple(jnp.concatenate(_lane_blocks(x, n), axis=0).astype(MXU_DTYPE) for x in (hi, lo))


def _unit_lower_inverse(lms, n):
    c = lms[0].shape[0]
    row = lax.broadcasted_iota(jnp.int32, (c, n * c), 0)
    lane = lax.broadcasted_iota(jnp.int32, (c, n * c), 1)
    eye = (jnp.bitwise_and(lane, c - 1) == row).astype(jnp.float32)
    ps = [-jnp.concatenate(lms[g:g + n], axis=1) for g in range(0, len(lms), n)]
    ts = [eye + p for p in ps]
    pbds = [_block_diag_split(p, n) for p in ps]
    for _ in range(int(math.log2(c)) - 1):
        ps = [_mm_x3(_split(p), pbd) for p, pbd in zip(ps, pbds)]
        pbds = [_block_diag_split(p, n) for p in ps]
        ts = [t + _mm_x3(_split(t), pbd) for t, pbd in zip(ts, pbds)]
    return ts


def _gdn_chunk_kernel(q_ref, k_ref, v_ref, ba_ref, a_ref, dt_ref, s0_ref,
                      o_ref, sout_ref, s_ref):
    d = pl.program_id(0)
    n = pl.program_id(2)
    c = GDN_CHUNK
    nb = q_ref.shape[0]

    @pl.when(n == 0)
    def _():
        for bb in range(nb):
            s_ref[bb * GDN_HEADS:(bb + 1) * GDN_HEADS] = s0_ref[0, bb]

    ii = lax.broadcasted_iota(jnp.int32, (c, c), 0)
    jj = lax.broadcasted_iota(jnp.int32, (c, c), 1)
    sgn = 1 - 2 * d
    diff = (ii - jj) * sgn
    incl = diff >= 0
    strict = diff > 0
    tri = incl.astype(jnp.float32)

    beta_alls, gc_alls, gc_all_ts, eg_alls, er_alls, el_alls = [], [], [], [], [], []
    for bb in range(nb):
        ba = ba_ref[bb]
        g_all = -a_ref[0] * _softplus(ba + dt_ref[0])
        gc_all = _mm_f32(tri, g_all)
        gtot_all = jnp.sum(g_all, axis=0, keepdims=True)
        beta_alls.append(_sigmoid(ba))
        gc_alls.append(gc_all)
        gc_all_ts.append(gc_all.T)
        eg_alls.append(jnp.exp(gc_all))
        er_alls.append(jnp.exp(gtot_all - gc_all))
        el_alls.append(jnp.exp(gtot_all))

    units = [(bb, h) for bb in range(nb) for h in range(GDN_HEADS)]

    def col(tables, bb, lane):
        return tables[bb][:, lane:lane + 1]

    sls = [slice(h * GDN_DK, (h + 1) * GDN_DK) for _, h in units]
    ks = [k_ref[bb, :, sl] for (bb, _), sl in zip(units, sls)]
    kbs = [k * col(beta_alls, bb, h) for (bb, h), k in zip(units, ks)]
    decs = [jnp.exp(jnp.where(incl, col(gc_alls, bb, GDN_HEADS + h)
                              - gc_all_ts[bb][GDN_HEADS + h:GDN_HEADS + h + 1, :], -jnp.inf))
            for bb, h in units]
    lms = [jnp.where(strict, _mm_nt(kb, k) * dec, 0.0) for kb, k, dec in zip(kbs, ks, decs)]
    us, ws = [], []
    tinvs = _unit_lower_inverse(lms, GDN_PACK)
    for g0, tinv in zip(range(0, len(units), GDN_PACK), tinvs):
        grp = range(g0, g0 + GDN_PACK)
        vb = jnp.concatenate([v_ref[units[x][0], :, sls[x]] * col(beta_alls, *units[x])
                              for x in grp], axis=0)
        kg = jnp.concatenate([kbs[x] * col(eg_alls, units[x][0], GDN_HEADS + units[x][1])
                              for x in grp], axis=0)
        for t_x in _lane_blocks(tinv, GDN_PACK):
            us.append(_mm(t_x, vb))
            ws.append(_mm(t_x, kg))
    qs = [q_ref[bb, :, sl] for (bb, _), sl in zip(units, sls)]
    attns = [jnp.where(incl, _mm_nt(q, k) * dec, 0.0) for q, k, dec in zip(qs, ks, decs)]
    ss = [s_ref[bb * GDN_HEADS + h] for bb, h in units]
    v_news = [u - _mm(w, s) for u, w, s in zip(us, ws, ss)]
    for x, (bb, h) in enumerate(units):
        qg = qs[x] * col(eg_alls, bb, GDN_HEADS + h)
        o_ref[0, bb, :, sls[x]] = _mm(qg, ss[x]) + _mm(attns[x], v_news[x])
    for x, (bb, h) in enumerate(units):
        kd = ks[x] * col(er_alls, bb, GDN_HEADS + h)
        s_ref[bb * GDN_HEADS + h] = (ss[x] * col(el_alls, bb, GDN_HEADS + h)
                                     + _mm_tn(kd, v_news[x]))

    @pl.when(n == pl.num_programs(2) - 1)
    def _():
        for bb in range(nb):
            sout_ref[0, bb] = s_ref[bb * GDN_HEADS:(bb + 1) * GDN_HEADS]


def gdn_chunks(qkv, p, ba_tile0, a_rows, dt_rows, s0, *, nb):
    b, l, _ = qkv.shape
    nc = l // GDN_CHUNK
    nb = math.gcd(nb, b)

    def row(d, bi, n):
        return n + d * (nc - 1 - 2 * n)

    return pl.pallas_call(
        _gdn_chunk_kernel,
        grid=(2, b // nb, nc),
        in_specs=[
            pl.BlockSpec((nb, GDN_CHUNK, GDN_W), lambda d, bi, n: (bi, row(d, bi, n), 0)),
            pl.BlockSpec((nb, GDN_CHUNK, GDN_W), lambda d, bi, n: (bi, row(d, bi, n), 1)),
            pl.BlockSpec((nb, GDN_CHUNK, GDN_W), lambda d, bi, n: (bi, row(d, bi, n), 2)),
            pl.BlockSpec((nb, GDN_CHUNK, LANES), lambda d, bi, n: (bi, row(d, bi, n), ba_tile0 + d)),
            pl.BlockSpec((1, 1, LANES), lambda d, bi, n: (d, 0, 0)),
            pl.BlockSpec((1, 1, LANES), lambda d, bi, n: (d, 0, 0)),
            pl.BlockSpec((1, nb, GDN_HEADS, GDN_DK, GDN_DV), lambda d, bi, n: (d, bi, 0, 0, 0)),
        ],
        out_specs=[
            pl.BlockSpec((1, nb, GDN_CHUNK, GDN_W), lambda d, bi, n: (d, bi, row(d, bi, n), 0)),
            pl.BlockSpec((1, nb, GDN_HEADS, GDN_DK, GDN_DV), lambda d, bi, n: (d, bi, 0, 0, 0)),
        ],
        out_shape=[
            jax.ShapeDtypeStruct((2, b, l, GDN_W), jnp.float32),
            jax.ShapeDtypeStruct((2, b, GDN_HEADS, GDN_DK, GDN_DV), jnp.float32),
        ],
        scratch_shapes=[pltpu.VMEM((nb * GDN_HEADS, GDN_DK, GDN_DV), jnp.float32)],
        compiler_params=_cparams(("parallel", "parallel", "arbitrary"), 40),
        name="gdn_chunks",
    )(qkv, qkv, qkv, p, a_rows, dt_rows, s0)


def _gated_out_kernel(of_ref, ob_ref, z_ref, w_ref, y_ref):
    for h in range(GDN_HEADS):
        sl = slice(h * GDN_DV, (h + 1) * GDN_DV)
        o = of_ref[0, 0, :, sl] + ob_ref[0, 0, :, sl]
        y = o * lax.rsqrt(jnp.mean(o * o, axis=-1, keepdims=True) + EPS)
        y_ref[0, :, sl] = (y * w_ref[...] * _silu(z_ref[0, :, sl])).astype(y_ref.dtype)


def gated_out(o, p, z_tile, w_norm, *, tm):
    _, b, l, _ = o.shape
    tm = min(tm, l)
    return pl.pallas_call(
        _gated_out_kernel,
        grid=(b, l // tm),
        in_specs=[
            pl.BlockSpec((1, 1, tm, GDN_W), lambda bi, i: (0, bi, i, 0)),
            pl.BlockSpec((1, 1, tm, GDN_W), lambda bi, i: (1, bi, i, 0)),
            pl.BlockSpec((1, tm, GDN_W), lambda bi, i: (bi, i, z_tile)),
            pl.BlockSpec((1, GDN_DV), lambda bi, i: (0, 0)),
        ],
        out_specs=pl.BlockSpec((1, tm, GDN_W), lambda bi, i: (bi, i, 0)),
        out_shape=jax.ShapeDtypeStruct((b, l, GDN_W), MXU_DTYPE),
        compiler_params=_cparams(("parallel", "parallel"), 40),
        name="gated_out",
    )(o, o, p, w_norm.reshape(1, GDN_DV))


def _fourier_kernel(x_ref, bd_ref, cs_ref, o_ref, z_ref):
    i = pl.program_id(1)
    l = x_ref.shape[1]

    @pl.when(i == 0)
    def _():
        z = _mm(x_ref[0], bd_ref[...])
        z_ref[0:l, :] = z[:, :FNET_W].astype(z_ref.dtype)
        z_ref[l:2 * l, :] = z[:, FNET_W:].astype(z_ref.dtype)

    o_ref[0] = jnp.dot(cs_ref[...], z_ref[...],
                       preferred_element_type=jnp.float32).astype(o_ref.dtype)


def _dft_tables(l):
    def cs(n, scale):
        k = jnp.arange(n, dtype=jnp.int32)
        ang = ((k[:, None] * k[None, :]) % n).astype(jnp.float32) * (2.0 * math.pi / n)
        return jnp.cos(ang) * scale, jnp.sin(ang) * scale

    c_l, s_l = cs(l, 1.0)
    c_g, s_g = cs(FNET_GW, 1.0 / math.sqrt(l * FNET_GW))
    eye = jnp.eye(FNET_GROUPS, dtype=jnp.float32)
    bd = jnp.concatenate([jnp.kron(eye, c_g), jnp.kron(eye, s_g)], axis=1)
    cs_l = jnp.concatenate([c_l, -s_l], axis=1)
    return bd.astype(MXU_DTYPE), cs_l.astype(MXU_DTYPE)


def fourier_mix(p, f_tile, *, tm):
    b, l, _ = p.shape
    tm = min(tm, l)
    bd, cs_l = _dft_tables(l)
    return pl.pallas_call(
        _fourier_kernel,
        grid=(b, l // tm),
        in_specs=[
            pl.BlockSpec((1, l, FNET_W), lambda bi, i: (bi, 0, f_tile)),
            pl.BlockSpec((FNET_W, 2 * FNET_W), lambda bi, i: (0, 0)),
            pl.BlockSpec((tm, 2 * l), lambda bi, i: (i, 0)),
        ],
        out_specs=pl.BlockSpec((1, tm, FNET_W), lambda bi, i: (bi, i, 0)),
        out_shape=jax.ShapeDtypeStruct((b, l, FNET_W), MXU_DTYPE),
        scratch_shapes=[pltpu.VMEM((2 * l, FNET_W), MXU_DTYPE)],
        compiler_params=_cparams(("parallel", "arbitrary"), 40),
        name="fourier_mix",
    )(p, bd, cs_l)


def _swa_kernel(*refs, use_window):
    if use_window:
        sink_ref, q_ref, k_ref, v_ref, kc_ref, vc_ref, o_ref = refs
    else:
        sink_ref, q_ref, kc_ref, vc_ref, o_ref = refs
    i = pl.program_id(1)
    tq = q_ref.shape[1]
    scale = SWA_DH ** -0.5
    if use_window:
        l = k_ref.shape[1]
        span = SWA_BLOCK + 2 * SWA_WINDOW
        start = i * tq
        base = pl.multiple_of(jnp.clip(start - SWA_WINDOW, 0, l - span), SWA_BLOCK)
        qpos = start + lax.broadcasted_iota(jnp.int32, (tq, span), 0)
        kpos = base + lax.broadcasted_iota(jnp.int32, (tq, span), 1)
        band = jnp.abs(qpos - kpos) <= SWA_WINDOW
    for g in range(SWA_KV):
        ksl = slice(g * SWA_DH, (g + 1) * SWA_DH)
        qg = jnp.concatenate(
            [q_ref[0, :, (g * SWA_GROUP + hh) * SWA_DH:(g * SWA_GROUP + hh + 1) * SWA_DH]
             for hh in range(SWA_GROUP)], axis=0)
        kc = kc_ref[0, :, ksl]
        vc = vc_ref[0, :, ksl]
        lc_all = _mm_nt(qg, kc) * scale
        if use_window:
            kw = k_ref[0, pl.ds(base, span), ksl]
            vw = v_ref[0, pl.ds(base, span), ksl]
            lw_all = _mm_nt(qg, kw) * scale
        outs = []
        for hh in range(SWA_GROUP):
            rs = slice(hh * tq, (hh + 1) * tq)
            sink = sink_ref[g * SWA_GROUP + hh]
            lc = lc_all[rs]
            m = jnp.maximum(jnp.max(lc, axis=-1, keepdims=True), sink)
            if use_window:
                lw = jnp.where(band, lw_all[rs], -jnp.inf)
                m = jnp.maximum(m, jnp.max(lw, axis=-1, keepdims=True))
                pw = jnp.exp(lw - m)
            pc = jnp.exp(lc - m)
            den = jnp.sum(pc, axis=-1, keepdims=True) + jnp.exp(sink - m)
            acc = _mm(pc, vc)
            if use_window:
                den = den + jnp.sum(pw, axis=-1, keepdims=True)
                acc = acc + _mm(pw, vw)
            outs.append(acc / den)
        for pair in range(SWA_GROUP // 2):
            lo = (g * SWA_GROUP + 2 * pair) * SWA_DH
            o_ref[0, :, lo:lo + 2 * SWA_DH] = jnp.concatenate(
                [outs[2 * pair], outs[2 * pair + 1]], axis=1).astype(o_ref.dtype)


def swa_attention(qkv, ckv, sinks, *, use_window):
    b, l, _ = qkv.shape
    c = ckv.shape[1]
    tq = SWA_BLOCK
    kt = SWA_Q_W // SWA_KV_W
    in_specs = [
        pl.BlockSpec(memory_space=pltpu.SMEM),
        pl.BlockSpec((1, tq, SWA_Q_W), lambda bi, i: (bi, i, 0)),
    ]
    args = [sinks, qkv]
    if use_window:
        in_specs += [
            pl.BlockSpec((1, l, SWA_KV_W), lambda bi, i: (bi, 0, kt)),
            pl.BlockSpec((1, l, SWA_KV_W), lambda bi, i: (bi, 0, kt + 1)),
        ]
        args += [qkv, qkv]
    in_specs += [
        pl.BlockSpec((1, c, SWA_KV_W), lambda bi, i: (bi, 0, 0)),
        pl.BlockSpec((1, c, SWA_KV_W), lambda bi, i: (bi, 0, 1)),
    ]
    args += [ckv, ckv]
    return pl.pallas_call(
        functools.partial(_swa_kernel, use_window=use_window),
        grid=(b, l // tq),
        in_specs=in_specs,
        out_specs=pl.BlockSpec((1, tq, SWA_Q_W), lambda bi, i: (bi, i, 0)),
        out_shape=jax.ShapeDtypeStruct((b, l, SWA_Q_W), MXU_DTYPE),
        compiler_params=_cparams(("parallel", "parallel"), 40),
        name="swa_attention",
    )(*args)


def _rope_tables(l):
    rows = l // GRID_W
    r = jnp.broadcast_to(jnp.arange(rows)[:, None], (rows, GRID_W)).reshape(-1).astype(jnp.float32)
    col = jnp.broadcast_to(jnp.arange(GRID_W)[None, :], (rows, GRID_W)).reshape(-1).astype(jnp.float32)
    n = SWA_DH // 4
    freq = ROPE_BASE ** (-jnp.arange(n, dtype=jnp.float32) / n)
    ang_r = r[:, None] * freq
    ang_c = col[:, None] * freq
    cos = jnp.concatenate([jnp.cos(ang_r)] * 2 + [jnp.cos(ang_c)] * 2, axis=-1)
    sin = jnp.concatenate([-jnp.sin(ang_r), jnp.sin(ang_r), -jnp.sin(ang_c), jnp.sin(ang_c)], axis=-1)
    reps = LANES // SWA_DH
    return jnp.tile(cos, (1, reps)), jnp.tile(sin, (1, reps))


def _sort_desc(rows):
    n = len(rows)
    assert n & (n - 1) == 0
    rows = list(rows)
    k = 2
    while k <= n:
        j = k // 2
        while j >= 1:
            for i in range(n):
                p = i ^ j
                if p > i:
                    hi, lo = jnp.maximum(rows[i], rows[p]), jnp.minimum(rows[i], rows[p])
                    rows[i], rows[p] = (hi, lo) if (i & k) == 0 else (lo, hi)
            j //= 2
        k *= 2
    return rows


def _top_values(s, count):
    levels = _sort_desc([s[SUBLANES * l:SUBLANES * (l + 1)] for l in range(s.shape[0] // SUBLANES)])
    levels = levels[:count]
    vals = []
    for k in range(count):
        m = jnp.max(levels[0], axis=0, keepdims=True)
        vals.append(m)
        hit = levels[0] == m
        keep = max(count - k - 1, 1)
        below = levels[1:] + [jnp.full_like(levels[0], -jnp.inf)]
        levels = [jnp.where(hit, b, a) for a, b in zip(levels[:keep], below[:keep])]
    return vals


def _peer_cells():
    n = PEER_TOPK + 1
    return [(a, b) for a in range(n) for b in range(n) if (a + 1) * (b + 1) <= n]


def _peer_topk_kernel(q_ref, keys_ref, s1_ref, e1_ref, d_ref, e0_ref):
    n = PEER_TOPK + 1
    cells = _peer_cells()
    for h in range(PEER_HEADS):
        q0 = q_ref[:, (2 * h) * PEER_HALF:(2 * h + 1) * PEER_HALF]
        q1 = q_ref[:, (2 * h + 1) * PEER_HALF:(2 * h + 2) * PEER_HALF]
        s0 = lax.dot_general(keys_ref[0], q0, _NT, precision=_HI,
                             preferred_element_type=jnp.float32)
        s1 = lax.dot_general(keys_ref[1], q1, _NT, precision=_HI,
                             preferred_element_type=jnp.float32)
        top0 = _top_values(s0, n)
        top1 = _top_values(s1, n)
        rows = [top0[a] + top1[b] for a, b in cells]
        rows += [jnp.full_like(rows[0], -jnp.inf)] * (-len(rows) % (8 * SUBLANES))
        best = _top_values(jnp.concatenate(rows, axis=0), n)
        thr = 0.5 * (best[PEER_TOPK - 1] + best[PEER_TOPK])
        z = jnp.ones_like(thr)
        for kk in range(1, PEER_TOPK):
            z = z + jnp.exp(best[kk] - best[0])
        s1_ref[h] = s1
        e1_ref[h] = jnp.exp(s1 - top1[0]) / z
        d_ref[h] = thr - s0
        e0_ref[h] = jnp.exp(s0 - top0[0])


def peer_topk(q, keys, *, tt):
    t = q.shape[0]
    tt = min(tt, t)
    out = jax.ShapeDtypeStruct((PEER_HEADS, PEER_NKEYS, t), jnp.float32)
    ospec = pl.BlockSpec((PEER_HEADS, PEER_NKEYS, tt), lambda i: (0, 0, i))
    return pl.pallas_call(
        _peer_topk_kernel,
        grid=(t // tt,),
        in_specs=[
            pl.BlockSpec((tt, q.shape[1]), lambda i: (i, 0)),
            pl.BlockSpec((2, PEER_NKEYS, PEER_HALF), lambda i: (0, 0, 0)),
        ],
        out_specs=[ospec] * 4,
        out_shape=[out] * 4,
        compiler_params=_cparams(("parallel",), 40),
        name="peer_topk",
    )(q, keys)


def _gelu(x):
    return 0.5 * x * (1.0 + lax.erf(x * (2.0 ** -0.5)))


def _peer_dense_kernel(*refs, final_norm):
    (hmt_ref, u0_ref, ua_ref, ub_ref, vt_ref, s1_ref, e1_ref, d_ref, e0_ref, x_ref,
     gate_ref) = refs[:11]
    pos = 11
    if final_norm:
        fn_ref = refs[pos]
        pos += 1
    o_ref, acc_ref, pt_ref, at_ref = refs[pos:pos + 4]
    e = pl.program_id(1)
    eg, tm = at_ref.shape
    ni = eg // PEER_NKEYS

    def first_matmul(u_ref):
        return jnp.dot(u_ref[...], hmt_ref[...], preferred_element_type=jnp.float32)

    @pl.when(e == 0)
    def _():
        acc_ref[...] = jnp.zeros_like(acc_ref)
        at_ref[...] = first_matmul(u0_ref)

    def gate_group(r, a_block):
        for il in range(ni):
            ig = r * ni + il
            rs = slice(ig * PEER_NKEYS, (ig + 1) * PEER_NKEYS)
            for tl in range(tm // LANES):
                cs = slice(tl * LANES, (tl + 1) * LANES)
                g = None
                for h in range(PEER_HEADS):
                    drow = d_ref[h, ig:ig + 1, cs]
                    erow = e0_ref[h, ig:ig + 1, cs]
                    term = jnp.where(s1_ref[h, :, cs] >= drow, e1_ref[h, :, cs] * erow, 0.0)
                    g = term if g is None else g + term
                pt_ref[rs, cs] = (g * _gelu(a_block(il, cs))).astype(pt_ref.dtype)

    def second_matmul(r):
        es = slice(r * eg, (r + 1) * eg)
        acc_ref[...] += jnp.dot(vt_ref[:, es], pt_ref[es, :], preferred_element_type=jnp.float32)

    a1 = first_matmul(ua_ref)
    gate_group(0, lambda il, cs: at_ref[il * PEER_NKEYS:(il + 1) * PEER_NKEYS, cs])
    second_matmul(0)
    at_ref[...] = first_matmul(ub_ref)
    gate_group(1, lambda il, cs: a1[il * PEER_NKEYS:(il + 1) * PEER_NKEYS, cs])
    second_matmul(1)

    @pl.when(e == pl.num_programs(1) - 1)
    def _():
        y = x_ref[...] + gate_ref[0] * acc_ref[...].T
        if final_norm:
            y = y * lax.rsqrt(jnp.mean(y * y, axis=-1, keepdims=True) + EPS) * fn_ref[...]
        o_ref[...] = y


def peer_dense(hmt, u, vt, layer, sel, x, gate, rows_per_gate, final_gain=None, *, tm, eg):
    t, d = x.shape
    ne = u.shape[1]
    tm = min(tm, t, rows_per_gate)
    te = 2 * eg
    assert vt.shape[1:] == (ne // te, d, te)
    assert rows_per_gate % tm == 0 and t % tm == 0 and ne % te == 0 and eg % PEER_NKEYS == 0
    ni = te // PEER_NKEYS
    assert ni % SUBLANES == 0
    s1, e1, dd, e0 = sel
    gpb = rows_per_gate // tm
    last = ne // eg - 1
    once = pl.Buffered(1)
    full = pl.BlockSpec((PEER_HEADS, PEER_NKEYS, tm), lambda i, e: (0, 0, i))
    rowsp = pl.BlockSpec((PEER_HEADS, ni, tm), lambda i, e: (0, e, i))
    in_specs = [
        pl.BlockSpec((d, tm), lambda i, e: (0, i)),
        pl.BlockSpec((None, eg, d), lambda i, e: (layer, 0, 0), pipeline_mode=once),
        pl.BlockSpec((None, eg, d), lambda i, e: (layer, 2 * e + 1, 0)),
        pl.BlockSpec((None, eg, d), lambda i, e: (layer, jnp.minimum(2 * e + 2, last), 0)),
        pl.BlockSpec((None, None, d, te), lambda i, e: (layer, e, 0, 0)),
        full, full, rowsp, rowsp,
        pl.BlockSpec((tm, d), lambda i, e: (i, 0), pipeline_mode=once),
        pl.BlockSpec((1, 1, d), lambda i, e: (i // gpb, 0, 0)),
    ]
    args = [hmt, u, u, u, vt, s1, e1, dd, e0, x, gate.reshape(gate.shape[0], 1, d)]
    if final_gain is not None:
        in_specs.append(pl.BlockSpec((1, d), lambda i, e: (0, 0)))
        args.append(final_gain.reshape(1, d))
    return pl.pallas_call(
        functools.partial(_peer_dense_kernel, final_norm=final_gain is not None),
        grid=(t // tm, ne // te),
        in_specs=in_specs,
        out_specs=pl.BlockSpec((tm, d), lambda i, e: (i, 0)),
        out_shape=jax.ShapeDtypeStruct((t, d), jnp.float32),
        scratch_shapes=[pltpu.VMEM((d, tm), jnp.float32), pltpu.VMEM((te, tm), MXU_DTYPE),
                        pltpu.VMEM((eg, tm), jnp.float32)],
        compiler_params=_cparams(("arbitrary", "arbitrary"), 60),
        name="peer_dense",
    )(*args)


def peer_block(x, gain, shift, scale, gate, w_q, keys, u, vt, layer, final_gain=None):
    b, l, d = x.shape
    q, hmt = mod_matmul(x, gain, shift, scale, w_q, tm=1024, tn=512, emit_a=True)
    sel = peer_topk(q.reshape(b * l, d), keys, tt=256)
    out = peer_dense(hmt, u, vt, layer, sel, x.reshape(b * l, d), gate, l,
                     final_gain, tm=256, eg=PEER_EXPERT_GROUP)
    return out.reshape(b, l, d)


def _pack_even_w_in(w_in):
    d = w_in.shape[0]
    o = 4 * GDN_W
    h = GDN_HEADS
    ba = w_in[:, o:o + 4 * h]
    zeros = jnp.zeros((d, LANES - 2 * h), w_in.dtype)
    fwd = jnp.concatenate([ba[:, 0:h], ba[:, 2 * h:3 * h], zeros], axis=1)
    bwd = jnp.concatenate([ba[:, h:2 * h], ba[:, 3 * h:4 * h], zeros], axis=1)
    pad = jnp.zeros((d, 2 * LANES), w_in.dtype)
    packed = jnp.concatenate([w_in[:, :o], w_in[:, o + 4 * h:], fwd, bwd, pad], axis=1)
    return packed.astype(MXU_DTYPE)


def _lane_rows(vals):
    z = jnp.zeros((2, GDN_HEADS), jnp.float32)
    pad = jnp.zeros((2, LANES - 2 * GDN_HEADS), jnp.float32)
    return jnp.concatenate([z, vals.astype(jnp.float32), pad], axis=1).reshape(2, 1, LANES)


def even_layer_mixer(x, h_c, m_lat, m_ctx, gain, w_in, conv_w, a_log, dt_bias, out_norm, w_out,
                     ctx_out):
    b = x.shape[0]
    w_packed = _pack_even_w_in(w_in)
    ba_tile0 = (4 * GDN_W + FNET_W) // LANES
    z_tile = 3
    f_tile = 4 * GDN_W // FNET_W
    a_rows = _lane_rows(jnp.exp(a_log.astype(jnp.float32)))
    dt_rows = _lane_rows(dt_bias)
    w_out_c = w_out.astype(MXU_DTYPE)
    w_y, w_f = w_out_c[:GDN_W], w_out_c[GDN_W:]

    p_ctx = mod_matmul(h_c, gain, m_ctx[0], m_ctx[1], w_packed, tm=512, tn=512)
    p_lat = mod_matmul(x, gain, m_lat[0], m_lat[1], w_packed, tm=1024, tn=512)
    zero = jnp.zeros((2, b, GDN_HEADS, GDN_DK, GDN_DV), jnp.float32)
    o_ctx, s_ctx = gdn_chunks(gdn_short_conv(p_ctx, conv_w), p_ctx, ba_tile0, a_rows, dt_rows, zero,
                              nb=2)
    o_lat, _ = gdn_chunks(gdn_short_conv(p_lat, conv_w), p_lat, ba_tile0, a_rows, dt_rows, s_ctx,
                          nb=2)

    def finish(p, o, resid, gate):
        y = gated_out(o, p, z_tile, out_norm, tm=512)
        f = fourier_mix(p, f_tile, tm=512)
        return resid_matmul([y, f], [w_y, w_f], resid, gate, tm=1024, tn=512)

    x = finish(p_lat, o_lat, x, m_lat[2])
    if ctx_out:
        h_c = finish(p_ctx, o_ctx, h_c, m_ctx[2])
    return x, h_c


def odd_layer_mixer(x, h_c, m_lat, m_ctx, gain, w_qkv, sinks, w_out, ctx_out):
    l = x.shape[1]
    w_c = w_qkv.astype(MXU_DTYPE)
    w_out_c = w_out.astype(MXU_DTYPE)
    rope = _rope_tables(l)
    tn = 256
    qkv = mod_matmul(x, gain, m_lat[0], m_lat[1], w_c, tm=1024, tn=tn,
                     rope=rope, rope_tiles=(SWA_Q_W + SWA_KV_W) // tn)
    ckv = mod_matmul(h_c, gain, m_ctx[0], m_ctx[1], w_c[:, SWA_Q_W:], tm=512, tn=tn)
    sinks = sinks.astype(jnp.float32)
    o = swa_attention(qkv, ckv, sinks, use_window=True)
    x_new = resid_matmul([o], [w_out_c], x, m_lat[2], tm=1024, tn=512)
    if ctx_out:
        q_c = mod_matmul(h_c, gain, m_ctx[0], m_ctx[1], w_c[:, :SWA_Q_W], tm=512, tn=tn)
        o_c = swa_attention(q_c, ckv, sinks, use_window=False)
        h_c = resid_matmul([o_c], [w_out_c], h_c, m_ctx[2], tm=1024, tn=512)
    return x_new, h_c


def kernel(x, c, ctx, c_ctx, ada_w, ada_b, norm_mix, norm_ffn, even_w_in, gdn_conv, gdn_a_log, gdn_dt_bias, gdn_out_norm, even_w_out, odd_w_qkv, odd_sinks, odd_w_out, peer_w_q, peer_keys, peer_u, peer_v, final_norm):
    b, _, d = x.shape
    depth = ada_w.shape[0]
    rows = 16
    cond = jnp.concatenate([c, c_ctx[None, :], jnp.zeros((rows - b - 1, d), c.dtype)], axis=0)
    mods = ada_mod_all(cond, ada_w, ada_b)
    conv_w_all = gdn_conv
    u_all = peer_u.astype(MXU_DTYPE)
    te = 2 * PEER_EXPERT_GROUP
    vt_all = peer_v.astype(MXU_DTYPE).reshape(depth, -1, te, d).transpose(0, 1, 3, 2)
    h_c = ctx
    for i in range(depth):
        last = i == depth - 1
        j = i // 2
        m = mods[i].reshape(rows, N_MOD, d)
        m_lat = [m[:b, k] for k in range(N_MOD)]
        m_ctx = [jnp.broadcast_to(m[b, k][None, :], (b, d)) for k in range(N_MOD)]
        if i % 2 == 0:
            x, h_c = even_layer_mixer(x, h_c, m_lat, m_ctx, norm_mix[i], even_w_in[j], conv_w_all[j],
                                      gdn_a_log[j], gdn_dt_bias[j], gdn_out_norm[j], even_w_out[j],
                                      not last)
        else:
            x, h_c = odd_layer_mixer(x, h_c, m_lat, m_ctx, norm_mix[i], odd_w_qkv[j], odd_sinks[j],
                                     odd_w_out[j], not last)
        w_q = peer_w_q[i].astype(MXU_DTYPE)
        keys = peer_keys[i].astype(jnp.float32)
        x = peer_block(x, norm_ffn[i], m_lat[3], m_lat[4], m_lat[5], w_q, keys, u_all, vt_all, i,
                       final_norm if last else None)
        if not last:
            h_c = peer_block(h_c, norm_ffn[i], m_ctx[3], m_ctx[4], m_ctx[5], w_q, keys,
                             u_all, vt_all, i)
    return x
```

```python
import functools
import math

import jax
import jax.numpy as jnp
from jax import lax
from jax.experimental import pallas as pl
from jax.experimental.pallas import tpu as pltpu

N_MOD = 6
EPS = 1e-6
GDN_HEADS = 12
GDN_DK = 128
GDN_DV = 128
GDN_W = GDN_HEADS * GDN_DV
GDN_CONV = 5
GDN_CHUNK = 64
GDN_PACK = 2
GDN_CONV_HEADS = 4
FNET_GROUPS = 4
FNET_GW = 128
FNET_W = FNET_GROUPS * FNET_GW
GRID_W = 64
SWA_HEADS = 32
SWA_KV = 4
SWA_GROUP = SWA_HEADS // SWA_KV
SWA_DH = 64
SWA_WINDOW = 128
SWA_BLOCK = 128
ROPE_BASE = 10000.0
SWA_Q_W = SWA_HEADS * SWA_DH
SWA_KV_W = SWA_KV * SWA_DH
PEER_HEADS = 8
PEER_NKEYS = 128
PEER_HALF = 128
PEER_TOPK = 16

LANES = 128
SUBLANES = 8
VMEM_BYTES_V7X = 64 * 1024 * 1024
MXU_DTYPE = jnp.bfloat16

_HI = lax.Precision.HIGHEST
_NT = (((1,), (1,)), ((), ()))
_TN = (((0,), (0,)), ((), ()))


def _cparams(semantics, vmem_mb):
    assert vmem_mb * 1024 * 1024 < VMEM_BYTES_V7X
    return pltpu.CompilerParams(dimension_semantics=semantics,
                                vmem_limit_bytes=vmem_mb * 1024 * 1024)


def _mm(a, b):
    return jnp.dot(a.astype(MXU_DTYPE), b.astype(MXU_DTYPE),
                   preferred_element_type=jnp.float32)


def _mm_nt(a, b):
    return lax.dot_general(a.astype(MXU_DTYPE), b.astype(MXU_DTYPE), _NT,
                           preferred_element_type=jnp.float32)


def _mm_tn(a, b):
    return lax.dot_general(a.astype(MXU_DTYPE), b.astype(MXU_DTYPE), _TN,
                           preferred_element_type=jnp.float32)


def _mm_f32(a, b):
    return jnp.dot(a, b, precision=_HI, preferred_element_type=jnp.float32)


def _split(a):
    hi = a.astype(MXU_DTYPE)
    lo = (a - hi.astype(jnp.float32)).astype(MXU_DTYPE)
    return hi, lo


def _mm_x3(a, b):
    (ah, al), (bh, bl) = a, b
    dot = functools.partial(jnp.dot, preferred_element_type=jnp.float32)
    m = ah.shape[0]
    both = dot(jnp.concatenate([ah, al], axis=0), bh)
    return both[:m] + (dot(ah, bl) + both[m:])


def _sigmoid(x):
    return 1.0 / (1.0 + jnp.exp(-x))


def _silu(x):
    return x * _sigmoid(x)


def _softplus(x):
    return jnp.maximum(x, 0.0) + jnp.log1p(jnp.exp(-jnp.abs(x)))


def _ada_kernel(cond_ref, w_ref, b_ref, o_ref):
    a = _silu(cond_ref[...])
    o_ref[0] = _mm(a, w_ref[0]) + b_ref[0]


def ada_mod_all(cond, ada_w, ada_b, *, tn=1024):
    depth, d, n = ada_w.shape
    r = cond.shape[0]
    return pl.pallas_call(
        _ada_kernel,
        grid=(depth, n // tn),
        in_specs=[
            pl.BlockSpec((r, d), lambda l, j: (0, 0)),
            pl.BlockSpec((1, d, tn), lambda l, j: (l, 0, j)),
            pl.BlockSpec((1, 1, tn), lambda l, j: (l, 0, j)),
        ],
        out_specs=pl.BlockSpec((1, r, tn), lambda l, j: (l, 0, j)),
        out_shape=jax.ShapeDtypeStruct((depth, r, n), jnp.float32),
        compiler_params=_cparams(("parallel", "parallel"), 40),
        name="ada_mod",
    )(cond, ada_w, ada_b.reshape(depth, 1, n))


def _rope_swap(y):
    n = y.shape[-1]
    lane = lax.broadcasted_iota(jnp.int32, y.shape, 1)
    up = pltpu.roll(y, n - 16, 1)
    down = pltpu.roll(y, 16, 1)
    return jnp.where(lane % 32 < 16, up, down)


def _mod_matmul_kernel(*refs, rope_tiles, emit_a, tn):
    x_ref, gain_ref, shift_ref, scale_ref, w_ref = refs[:5]
    pos = 5
    if rope_tiles:
        cos_ref, sin_ref = refs[pos:pos + 2]
        pos += 2
    o_ref = refs[pos]
    pos += 1
    if emit_a:
        a_out_ref = refs[pos]
        pos += 1
    a_ref = refs[pos]
    j = pl.program_id(2)

    @pl.when(j == 0)
    def _():
        x = x_ref[0]
        y = x * lax.rsqrt(jnp.mean(x * x, axis=-1, keepdims=True) + EPS)
        y = y * gain_ref[...]
        a = y * (1.0 + scale_ref[0]) + shift_ref[0]
        a_ref[...] = a.astype(a_ref.dtype)
        if emit_a:
            a_out_ref[...] = a.T.astype(a_out_ref.dtype)

    acc = jnp.dot(a_ref[...], w_ref[...], preferred_element_type=jnp.float32)
    if rope_tiles:
        @pl.when(j < rope_tiles)
        def _():
            cos = cos_ref[...]
            sin = sin_ref[...]
            for t in range(tn // LANES):
                sl = slice(t * LANES, (t + 1) * LANES)
                y = acc[:, sl]
                o_ref[0, :, sl] = (y * cos + _rope_swap(y) * sin).astype(o_ref.dtype)

        @pl.when(j >= rope_tiles)
        def _():
            o_ref[0] = acc.astype(o_ref.dtype)
    else:
        o_ref[0] = acc.astype(o_ref.dtype)


def mod_matmul(x, gain, shift, scale, w, *, tm, tn, rope=None, rope_tiles=0, emit_a=False):
    b, l, d = x.shape
    n = w.shape[1]
    tm = min(tm, l)
    assert l % tm == 0 and n % tn == 0 and tn % LANES == 0
    in_specs = [
        pl.BlockSpec((1, tm, d), lambda bi, i, j: (bi, i, 0)),
        pl.BlockSpec((1, d), lambda bi, i, j: (0, 0)),
        pl.BlockSpec((1, 1, d), lambda bi, i, j: (bi, 0, 0)),
        pl.BlockSpec((1, 1, d), lambda bi, i, j: (bi, 0, 0)),
        pl.BlockSpec((d, tn), lambda bi, i, j: (0, j)),
    ]
    args = [x, gain.reshape(1, d), shift.reshape(b, 1, d), scale.reshape(b, 1, d), w]
    if rope_tiles:
        in_specs += [pl.BlockSpec((tm, LANES), lambda bi, i, j: (i, 0))] * 2
        args += list(rope)
    out_specs = [pl.BlockSpec((1, tm, tn), lambda bi, i, j: (bi, i, j))]
    out_shape = [jax.ShapeDtypeStruct((b, l, n), jnp.float32)]
    if emit_a:
        nb = l // tm
        out_specs.append(pl.BlockSpec((d, tm), lambda bi, i, j: (0, bi * nb + i)))
        out_shape.append(jax.ShapeDtypeStruct((d, b * l), w.dtype))
    res = pl.pallas_call(
        functools.partial(_mod_matmul_kernel, rope_tiles=rope_tiles, emit_a=emit_a, tn=tn),
        grid=(b, l // tm, n // tn),
        in_specs=in_specs,
        out_specs=out_specs,
        out_shape=out_shape,
        scratch_shapes=[pltpu.VMEM((tm, d), w.dtype)],
        compiler_params=_cparams(("parallel", "parallel", "arbitrary"), 56),
        name="mod_matmul",
    )(*args)
    return res if emit_a else res[0]


def _resid_matmul_kernel(*refs, n_pairs):
    a_refs = refs[:n_pairs]
    w_refs = refs[n_pairs:2 * n_pairs]
    x_ref, gate_ref, o_ref = refs[2 * n_pairs:]
    acc = jnp.dot(a_refs[0][0], w_refs[0][...], preferred_element_type=jnp.float32)
    for a_ref, w_ref in zip(a_refs[1:], w_refs[1:]):
        acc += jnp.dot(a_ref[0], w_ref[...], preferred_element_type=jnp.float32)
    o_ref[0] = x_ref[0] + gate_ref[0] * acc


def resid_matmul(acts, weights, x, gate, *, tm, tn):
    b, l, n = x.shape
    tm = min(tm, l)
    in_specs = []
    for a in acts:
        in_specs.append(pl.BlockSpec((1, tm, a.shape[2]), lambda bi, i, j: (bi, i, 0)))
    for w in weights:
        in_specs.append(pl.BlockSpec((w.shape[0], tn), lambda bi, i, j: (0, j)))
    in_specs += [
        pl.BlockSpec((1, tm, tn), lambda bi, i, j: (bi, i, j)),
        pl.BlockSpec((1, 1, tn), lambda bi, i, j: (bi, 0, j)),
    ]
    return pl.pallas_call(
        functools.partial(_resid_matmul_kernel, n_pairs=len(acts)),
        grid=(b, l // tm, n // tn),
        in_specs=in_specs,
        out_specs=pl.BlockSpec((1, tm, tn), lambda bi, i, j: (bi, i, j)),
        out_shape=jax.ShapeDtypeStruct((b, l, n), jnp.float32),
        compiler_params=_cparams(("parallel", "parallel", "parallel"), 40),
        name="resid_matmul",
    )(*acts, *weights, x, gate.reshape(b, 1, n))


def _gdn_conv_kernel(p_ref, w_ref, o_ref):
    c = pl.program_id(1)
    l = p_ref.shape[1]
    row = lax.broadcasted_iota(jnp.int32, (l, GDN_DK), 0)
    pad = (GDN_CONV - 1) // 2
    for hh in range(GDN_CONV_HEADS):
        sl = slice(hh * GDN_DK, (hh + 1) * GDN_DK)
        head = c * GDN_CONV_HEADS + hh
        x = p_ref[0, :, sl]
        y = x * w_ref[0, pad:pad + 1, sl]
        for t in range(GDN_CONV):
            s = t - pad
            if s == 0:
                continue
            xs = pltpu.roll(x, (-s) % l, 0)
            ok = (row + s >= 0) & (row + s < l)
            y = y + jnp.where(ok, xs, 0.0) * w_ref[0, t:t + 1, sl]
        y = _silu(y)
        inv = lax.rsqrt(jnp.sum(y * y, axis=-1, keepdims=True) + 1e-6)
        fac = jnp.where(head < 2 * GDN_HEADS, inv, 1.0)
        fac = fac * jnp.where(head < GDN_HEADS, GDN_DK ** -0.5, 1.0)
        o_ref[0, :, sl] = y * fac


def gdn_short_conv(p, conv_w):
    b, l, _ = p.shape
    tw = GDN_CONV_HEADS * GDN_DK
    nt = 3 * GDN_W // tw
    w = conv_w.reshape(GDN_CONV, nt, tw).transpose(1, 0, 2)
    return pl.pallas_call(
        _gdn_conv_kernel,
        grid=(b, nt),
        in_specs=[
            pl.BlockSpec((1, l, tw), lambda bi, c: (bi, 0, c)),
            pl.BlockSpec((1, GDN_CONV, tw), lambda bi, c: (c, 0, 0)),
        ],
        out_specs=pl.BlockSpec((1, l, tw), lambda bi, c: (bi, 0, c)),
        out_shape=jax.ShapeDtypeStruct((b, l, 3 * GDN_W), jnp.float32),
        compiler_params=_cparams(("parallel", "parallel"), 40),
        name="gdn_conv",
    )(p, w)


def _lane_blocks(cat, n):
    shift = int(math.log2(cat.shape[1] // n))
    blk = jnp.right_shift(lax.broadcasted_iota(jnp.int32, cat.shape, 1), shift)
    return [jnp.where(blk == b, cat, 0.0) for b in range(n)]


def _block_diag_split(cat, n):
    hi = cat.astype(MXU_DTYPE).astype(jnp.float32)
    lo = cat - hi
    return tuple(jnp.concatenate(_lane_blocks(x, n), axis=0).astype(MXU_DTYPE) for x in (hi, lo))


def _unit_lower_inverse(lms, n):
    c = lms[0].shape[0]
    row = lax.broadcasted_iota(jnp.int32, (c, n * c), 0)
    lane = lax.broadcasted_iota(jnp.int32, (c, n * c), 1)
    eye = (jnp.bitwise_and(lane, c - 1) == row).astype(jnp.float32)
    ps = [-jnp.concatenate(lms[g:g + n], axis=1) for g in range(0, len(lms), n)]
    ts = [eye + p for p in ps]
    pbds = [_block_diag_split(p, n) for p in ps]
    for _ in range(int(math.log2(c)) - 1):
        ps = [_mm_x3(_split(p), pbd) for p, pbd in zip(ps, pbds)]
        pbds = [_block_diag_split(p, n) for p in ps]
        ts = [t + _mm_x3(_split(t), pbd) for t, pbd in zip(ts, pbds)]
    return ts


def _gdn_chunk_kernel(q_ref, k_ref, v_ref, ba_ref, a_ref, dt_ref, s0_ref,
                      o_ref, sout_ref, s_ref):
    d = pl.program_id(0)
    n = pl.program_id(2)
    c = GDN_CHUNK
    nb = q_ref.shape[0]

    @pl.when(n == 0)
    def _():
        for bb in range(nb):
            s_ref[bb * GDN_HEADS:(bb + 1) * GDN_HEADS] = s0_ref[0, bb]

    ii = lax.broadcasted_iota(jnp.int32, (c, c), 0)
    jj = lax.broadcasted_iota(jnp.int32, (c, c), 1)
    sgn = 1 - 2 * d
    diff = (ii - jj) * sgn
    incl = diff >= 0
    strict = diff > 0
    tri = incl.astype(jnp.float32)

    beta_alls, gc_alls, gc_all_ts, eg_alls, er_alls, el_alls = [], [], [], [], [], []
    for bb in range(nb):
        ba = ba_ref[bb]
        g_all = -a_ref[0] * _softplus(ba + dt_ref[0])
        gc_all = _mm_f32(tri, g_all)
        gtot_all = jnp.sum(g_all, axis=0, keepdims=True)
        beta_alls.append(_sigmoid(ba))
        gc_alls.append(gc_all)
        gc_all_ts.append(gc_all.T)
        eg_alls.append(jnp.exp(gc_all))
        er_alls.append(jnp.exp(gtot_all - gc_all))
        el_alls.append(jnp.exp(gtot_all))

    units = [(bb, h) for bb in range(nb) for h in range(GDN_HEADS)]

    def col(tables, bb, lane):
        return tables[bb][:, lane:lane + 1]

    sls = [slice(h * GDN_DK, (h + 1) * GDN_DK) for _, h in units]
    ks = [k_ref[bb, :, sl] for (bb, _), sl in zip(units, sls)]
    kbs = [k * col(beta_alls, bb, h) for (bb, h), k in zip(units, ks)]
    decs = [jnp.exp(jnp.where(incl, col(gc_alls, bb, GDN_HEADS + h)
                              - gc_all_ts[bb][GDN_HEADS + h:GDN_HEADS + h + 1, :], -jnp.inf))
            for bb, h in units]
    lms = [jnp.where(strict, _mm_nt(kb, k) * dec, 0.0) for kb, k, dec in zip(kbs, ks, decs)]
    us, ws = [], []
    tinvs = _unit_lower_inverse(lms, GDN_PACK)
    for g0, tinv in zip(range(0, len(units), GDN_PACK), tinvs):
        grp = range(g0, g0 + GDN_PACK)
        vb = jnp.concatenate([v_ref[units[x][0], :, sls[x]] * col(beta_alls, *units[x])
                              for x in grp], axis=0)
        kg = jnp.concatenate([kbs[x] * col(eg_alls, units[x][0], GDN_HEADS + units[x][1])
                              for x in grp], axis=0)
        for t_x in _lane_blocks(tinv, GDN_PACK):
            us.append(_mm(t_x, vb))
            ws.append(_mm(t_x, kg))
    qs = [q_ref[bb, :, sl] for (bb, _), sl in zip(units, sls)]
    attns = [jnp.where(incl, _mm_nt(q, k) * dec, 0.0) for q, k, dec in zip(qs, ks, decs)]
    ss = [s_ref[bb * GDN_HEADS + h] for bb, h in units]
    v_news = [u - _mm(w, s) for u, w, s in zip(us, ws, ss)]
    for x, (bb, h) in enumerate(units):
        qg = qs[x] * col(eg_alls, bb, GDN_HEADS + h)
        o_ref[0, bb, :, sls[x]] = _mm(qg, ss[x]) + _mm(attns[x], v_news[x])
    for x, (bb, h) in enumerate(units):
        kd = ks[x] * col(er_alls, bb, GDN_HEADS + h)
        s_ref[bb * GDN_HEADS + h] = (ss[x] * col(el_alls, bb, GDN_HEADS + h)
                                     + _mm_tn(kd, v_news[x]))

    @pl.when(n == pl.num_programs(2) - 1)
    def _():
        for bb in range(nb):
            sout_ref[0, bb] = s_ref[bb * GDN_HEADS:(bb + 1) * GDN_HEADS]


def gdn_chunks(qkv, p, ba_tile0, a_rows, dt_rows, s0, *, nb):
    b, l, _ = qkv.shape
    nc = l // GDN_CHUNK
    nb = math.gcd(nb, b)

    def row(d, bi, n):
        return n + d * (nc - 1 - 2 * n)

    return pl.pallas_call(
        _gdn_chunk_kernel,
        grid=(2, b // nb, nc),
        in_specs=[
            pl.BlockSpec((nb, GDN_CHUNK, GDN_W), lambda d, bi, n: (bi, row(d, bi, n), 0)),
            pl.BlockSpec((nb, GDN_CHUNK, GDN_W), lambda d, bi, n: (bi, row(d, bi, n), 1)),
            pl.BlockSpec((nb, GDN_CHUNK, GDN_W), lambda d, bi, n: (bi, row(d, bi, n), 2)),
            pl.BlockSpec((nb, GDN_CHUNK, LANES), lambda d, bi, n: (bi, row(d, bi, n), ba_tile0 + d)),
            pl.BlockSpec((1, 1, LANES), lambda d, bi, n: (d, 0, 0)),
            pl.BlockSpec((1, 1, LANES), lambda d, bi, n: (d, 0, 0)),
            pl.BlockSpec((1, nb, GDN_HEADS, GDN_DK, GDN_DV), lambda d, bi, n: (d, bi, 0, 0, 0)),
        ],
        out_specs=[
            pl.BlockSpec((1, nb, GDN_CHUNK, GDN_W), lambda d, bi, n: (d, bi, row(d, bi, n), 0)),
            pl.BlockSpec((1, nb, GDN_HEADS, GDN_DK, GDN_DV), lambda d, bi, n: (d, bi, 0, 0, 0)),
        ],
        out_shape=[
            jax.ShapeDtypeStruct((2, b, l, GDN_W), jnp.float32),
            jax.ShapeDtypeStruct((2, b, GDN_HEADS, GDN_DK, GDN_DV), jnp.float32),
        ],
        scratch_shapes=[pltpu.VMEM((nb * GDN_HEADS, GDN_DK, GDN_DV), jnp.float32)],
        compiler_params=_cparams(("parallel", "parallel", "arbitrary"), 40),
        name="gdn_chunks",
    )(qkv, qkv, qkv, p, a_rows, dt_rows, s0)


def _gated_out_kernel(of_ref, ob_ref, z_ref, w_ref, y_ref):
    for h in range(GDN_HEADS):
        sl = slice(h * GDN_DV, (h + 1) * GDN_DV)
        o = of_ref[0, 0, :, sl] + ob_ref[0, 0, :, sl]
        y = o * lax.rsqrt(jnp.mean(o * o, axis=-1, keepdims=True) + EPS)
        y_ref[0, :, sl] = (y * w_ref[...] * _silu(z_ref[0, :, sl])).astype(y_ref.dtype)


def gated_out(o, p, z_tile, w_norm, *, tm):
    _, b, l, _ = o.shape
    tm = min(tm, l)
    return pl.pallas_call(
        _gated_out_kernel,
        grid=(b, l // tm),
        in_specs=[
            pl.BlockSpec((1, 1, tm, GDN_W), lambda bi, i: (0, bi, i, 0)),
            pl.BlockSpec((1, 1, tm, GDN_W), lambda bi, i: (1, bi, i, 0)),
            pl.BlockSpec((1, tm, GDN_W), lambda bi, i: (bi, i, z_tile)),
            pl.BlockSpec((1, GDN_DV), lambda bi, i: (0, 0)),
        ],
        out_specs=pl.BlockSpec((1, tm, GDN_W), lambda bi, i: (bi, i, 0)),
        out_shape=jax.ShapeDtypeStruct((b, l, GDN_W), MXU_DTYPE),
        compiler_params=_cparams(("parallel", "parallel"), 40),
        name="gated_out",
    )(o, o, p, w_norm.reshape(1, GDN_DV))


def _fourier_kernel(x_ref, bd_ref, cs_ref, o_ref, z_ref):
    i = pl.program_id(1)
    l = x_ref.shape[1]

    @pl.when(i == 0)
    def _():
        z = _mm(x_ref[0], bd_ref[...])
        z_ref[0:l, :] = z[:, :FNET_W].astype(z_ref.dtype)
        z_ref[l:2 * l, :] = z[:, FNET_W:].astype(z_ref.dtype)

    o_ref[0] = jnp.dot(cs_ref[...], z_ref[...],
                       preferred_element_type=jnp.float32).astype(o_ref.dtype)


def _dft_tables(l):
    def cs(n, scale):
        k = jnp.arange(n, dtype=jnp.int32)
        ang = ((k[:, None] * k[None, :]) % n).astype(jnp.float32) * (2.0 * math.pi / n)
        return jnp.cos(ang) * scale, jnp.sin(ang) * scale

    c_l, s_l = cs(l, 1.0)
    c_g, s_g = cs(FNET_GW, 1.0 / math.sqrt(l * FNET_GW))
    eye = jnp.eye(FNET_GROUPS, dtype=jnp.float32)
    bd = jnp.concatenate([jnp.kron(eye, c_g), jnp.kron(eye, s_g)], axis=1)
    cs_l = jnp.concatenate([c_l, -s_l], axis=1)
    return bd.astype(MXU_DTYPE), cs_l.astype(MXU_DTYPE)


def fourier_mix(p, f_tile, *, tm):
    b, l, _ = p.shape
    tm = min(tm, l)
    bd, cs_l = _dft_tables(l)
    return pl.pallas_call(
        _fourier_kernel,
        grid=(b, l // tm),
        in_specs=[
            pl.BlockSpec((1, l, FNET_W), lambda bi, i: (bi, 0, f_tile)),
            pl.BlockSpec((FNET_W, 2 * FNET_W), lambda bi, i: (0, 0)),
            pl.BlockSpec((tm, 2 * l), lambda bi, i: (i, 0)),
        ],
        out_specs=pl.BlockSpec((1, tm, FNET_W), lambda bi, i: (bi, i, 0)),
        out_shape=jax.ShapeDtypeStruct((b, l, FNET_W), MXU_DTYPE),
        scratch_shapes=[pltpu.VMEM((2 * l, FNET_W), MXU_DTYPE)],
        compiler_params=_cparams(("parallel", "arbitrary"), 40),
        name="fourier_mix",
    )(p, bd, cs_l)


def _swa_kernel(*refs, use_window):
    if use_window:
        sink_ref, q_ref, k_ref, v_ref, kc_ref, vc_ref, o_ref = refs
    else:
        sink_ref, q_ref, kc_ref, vc_ref, o_ref = refs
    i = pl.program_id(1)
    tq = q_ref.shape[1]
    scale = SWA_DH ** -0.5
    if use_window:
        l = k_ref.shape[1]
        span = SWA_BLOCK + 2 * SWA_WINDOW
        start = i * tq
        base = pl.multiple_of(jnp.clip(start - SWA_WINDOW, 0, l - span), SWA_BLOCK)
        qpos = start + lax.broadcasted_iota(jnp.int32, (tq, span), 0)
        kpos = base + lax.broadcasted_iota(jnp.int32, (tq, span), 1)
        band = jnp.abs(qpos - kpos) <= SWA_WINDOW
    for g in range(SWA_KV):
        ksl = slice(g * SWA_DH, (g + 1) * SWA_DH)
        qg = jnp.concatenate(
            [q_ref[0, :, (g * SWA_GROUP + hh) * SWA_DH:(g * SWA_GROUP + hh + 1) * SWA_DH]
             for hh in range(SWA_GROUP)], axis=0)
        kc = kc_ref[0, :, ksl]
        vc = vc_ref[0, :, ksl]
        lc_all = _mm_nt(qg, kc) * scale
        if use_window:
            kw = k_ref[0, pl.ds(base, span), ksl]
            vw = v_ref[0, pl.ds(base, span), ksl]
            lw_all = _mm_nt(qg, kw) * scale
        outs = []
        for hh in range(SWA_GROUP):
            rs = slice(hh * tq, (hh + 1) * tq)
            sink = sink_ref[g * SWA_GROUP + hh]
            lc = lc_all[rs]
            m = jnp.maximum(jnp.max(lc, axis=-1, keepdims=True), sink)
            if use_window:
                lw = jnp.where(band, lw_all[rs], -jnp.inf)
                m = jnp.maximum(m, jnp.max(lw, axis=-1, keepdims=True))
                pw = jnp.exp(lw - m)
            pc = jnp.exp(lc - m)
            den = jnp.sum(pc, axis=-1, keepdims=True) + jnp.exp(sink - m)
            acc = _mm(pc, vc)
            if use_window:
                den = den + jnp.sum(pw, axis=-1, keepdims=True)
                acc = acc + _mm(pw, vw)
            outs.append(acc / den)
        for pair in range(SWA_GROUP // 2):
            lo = (g * SWA_GROUP + 2 * pair) * SWA_DH
            o_ref[0, :, lo:lo + 2 * SWA_DH] = jnp.concatenate(
                [outs[2 * pair], outs[2 * pair + 1]], axis=1).astype(o_ref.dtype)


def swa_attention(qkv, ckv, sinks, *, use_window):
    b, l, _ = qkv.shape
    c = ckv.shape[1]
    tq = SWA_BLOCK
    kt = SWA_Q_W // SWA_KV_W
    in_specs = [
        pl.BlockSpec(memory_space=pltpu.SMEM),
        pl.BlockSpec((1, tq, SWA_Q_W), lambda bi, i: (bi, i, 0)),
    ]
    args = [sinks, qkv]
    if use_window:
        in_specs += [
            pl.BlockSpec((1, l, SWA_KV_W), lambda bi, i: (bi, 0, kt)),
            pl.BlockSpec((1, l, SWA_KV_W), lambda bi, i: (bi, 0, kt + 1)),
        ]
        args += [qkv, qkv]
    in_specs += [
        pl.BlockSpec((1, c, SWA_KV_W), lambda bi, i: (bi, 0, 0)),
        pl.BlockSpec((1, c, SWA_KV_W), lambda bi, i: (bi, 0, 1)),
    ]
    args += [ckv, ckv]
    return pl.pallas_call(
        functools.partial(_swa_kernel, use_window=use_window),
        grid=(b, l // tq),
        in_specs=in_specs,
        out_specs=pl.BlockSpec((1, tq, SWA_Q_W), lambda bi, i: (bi, i, 0)),
        out_shape=jax.ShapeDtypeStruct((b, l, SWA_Q_W), MXU_DTYPE),
        compiler_params=_cparams(("parallel", "parallel"), 40),
        name="swa_attention",
    )(*args)


def _rope_tables(l):
    rows = l // GRID_W
    r = jnp.broadcast_to(jnp.arange(rows)[:, None], (rows, GRID_W)).reshape(-1).astype(jnp.float32)
    col = jnp.broadcast_to(jnp.arange(GRID_W)[None, :], (rows, GRID_W)).reshape(-1).astype(jnp.float32)
    n = SWA_DH // 4
    freq = ROPE_BASE ** (-jnp.arange(n, dtype=jnp.float32) / n)
    ang_r = r[:, None] * freq
    ang_c = col[:, None] * freq
    cos = jnp.concatenate([jnp.cos(ang_r)] * 2 + [jnp.cos(ang_c)] * 2, axis=-1)
    sin = jnp.concatenate([-jnp.sin(ang_r), jnp.sin(ang_r), -jnp.sin(ang_c), jnp.sin(ang_c)], axis=-1)
    reps = LANES // SWA_DH
    return jnp.tile(cos, (1, reps)), jnp.tile(sin, (1, reps))


def _sort_desc(rows):
    n = len(rows)
    assert n & (n - 1) == 0
    rows = list(rows)
    k = 2
    while k <= n:
        j = k // 2
        while j >= 1:
            for i in range(n):
                p = i ^ j
                if p > i:
                    hi, lo = jnp.maximum(rows[i], rows[p]), jnp.minimum(rows[i], rows[p])
                    rows[i], rows[p] = (hi, lo) if (i & k) == 0 else (lo, hi)
            j //= 2
        k *= 2
    return rows


def _top_values(s, count):
    levels = _sort_desc([s[SUBLANES * l:SUBLANES * (l + 1)] for l in range(s.shape[0] // SUBLANES)])
    levels = levels[:count]
    vals = []
    for k in range(count):
        m = jnp.max(levels[0], axis=0, keepdims=True)
        vals.append(m)
        hit = levels[0] == m
        keep = max(count - k - 1, 1)
        below = levels[1:] + [jnp.full_like(levels[0], -jnp.inf)]
        levels = [jnp.where(hit, b, a) for a, b in zip(levels[:keep], below[:keep])]
    return vals


def _peer_cells():
    n = PEER_TOPK + 1
    return [(a, b) for a in range(n) for b in range(n) if (a + 1) * (b + 1) <= n]


def _peer_topk_kernel(q_ref, keys_ref, s1_ref, e1_ref, d_ref, e0_ref):
    n = PEER_TOPK + 1
    cells = _peer_cells()
    for h in range(PEER_HEADS):
        q0 = q_ref[:, (2 * h) * PEER_HALF:(2 * h + 1) * PEER_HALF]
        q1 = q_ref[:, (2 * h + 1) * PEER_HALF:(2 * h + 2) * PEER_HALF]
        s0 = lax.dot_general(keys_ref[0], q0, _NT, precision=_HI,
                             preferred_element_type=jnp.float32)
        s1 = lax.dot_general(keys_ref[1], q1, _NT, precision=_HI,
                             preferred_element_type=jnp.float32)
        top0 = _top_values(s0, n)
        top1 = _top_values(s1, n)
        rows = [top0[a] + top1[b] for a, b in cells]
        rows += [jnp.full_like(rows[0], -jnp.inf)] * (-len(rows) % (8 * SUBLANES))
        best = _top_values(jnp.concatenate(rows, axis=0), n)
        thr = 0.5 * (best[PEER_TOPK - 1] + best[PEER_TOPK])
        z = jnp.ones_like(thr)
        for kk in range(1, PEER_TOPK):
            z = z + jnp.exp(best[kk] - best[0])
        s1_ref[h] = s1
        e1_ref[h] = jnp.exp(s1 - top1[0]) / z
        d_ref[h] = thr - s0
        e0_ref[h] = jnp.exp(s0 - top0[0])


def peer_topk(q, keys, *, tt):
    t = q.shape[0]
    tt = min(tt, t)
    out = jax.ShapeDtypeStruct((PEER_HEADS, PEER_NKEYS, t), jnp.float32)
    ospec = pl.BlockSpec((PEER_HEADS, PEER_NKEYS, tt), lambda i: (0, 0, i))
    return pl.pallas_call(
        _peer_topk_kernel,
        grid=(t // tt,),
        in_specs=[
            pl.BlockSpec((tt, q.shape[1]), lambda i: (i, 0)),
            pl.BlockSpec((2, PEER_NKEYS, PEER_HALF), lambda i: (0, 0, 0)),
        ],
        out_specs=[ospec] * 4,
        out_shape=[out] * 4,
        compiler_params=_cparams(("parallel",), 40),
        name="peer_topk",
    )(q, keys)


def _gelu(x):
    return 0.5 * x * (1.0 + lax.erf(x * (2.0 ** -0.5)))


def _peer_dense_kernel(*refs, final_norm):
    (hmt_ref, u0_ref, ua_ref, ub_ref, vt_ref, s1_ref, e1_ref, d_ref, e0_ref, x_ref,
     gate_ref) = refs[:11]
    pos = 11
    if final_norm:
        fn_ref = refs[pos]
        pos += 1
    o_ref, acc_ref, pt_ref, at_ref = refs[pos:pos + 4]
    e = pl.program_id(1)
    eg, tm = at_ref.shape
    ni = eg // PEER_NKEYS

    def first_matmul(u_ref):
        return jnp.dot(u_ref[...], hmt_ref[...], preferred_element_type=jnp.float32)

    @pl.when(e == 0)
    def _():
        acc_ref[...] = jnp.zeros_like(acc_ref)
        at_ref[...] = first_matmul(u0_ref)

    def gate_group(r, a_block):
        for il in range(ni):
            ig = r * ni + il
            rs = slice(ig * PEER_NKEYS, (ig + 1) * PEER_NKEYS)
            for tl in range(tm // LANES):
                cs = slice(tl * LANES, (tl + 1) * LANES)
                g = None
                for h in range(PEER_HEADS):
                    drow = d_ref[h, ig:ig + 1, cs]
                    erow = e0_ref[h, ig:ig + 1, cs]
                    term = jnp.where(s1_ref[h, :, cs] >= drow, e1_ref[h, :, cs] * erow, 0.0)
                    g = term if g is None else g + term
                pt_ref[rs, cs] = (g * _gelu(a_block(il, cs))).astype(pt_ref.dtype)

    def second_matmul(r):
        es = slice(r * eg, (r + 1) * eg)
        acc_ref[...] += jnp.dot(vt_ref[:, es], pt_ref[es, :], preferred_element_type=jnp.float32)

    a1 = first_matmul(ua_ref)
    gate_group(0, lambda il, cs: at_ref[il * PEER_NKEYS:(il + 1) * PEER_NKEYS, cs])
    second_matmul(0)
    at_ref[...] = first_matmul(ub_ref)
    gate_group(1, lambda il, cs: a1[il * PEER_NKEYS:(il + 1) * PEER_NKEYS, cs])
    second_matmul(1)

    @pl.when(e == pl.num_programs(1) - 1)
    def _():
        y = x_ref[...] + gate_ref[0] * acc_ref[...].T
        if final_norm:
            y = y * lax.rsqrt(jnp.mean(y * y, axis=-1, keepdims=True) + EPS) * fn_ref[...]
        o_ref[...] = y


def peer_dense(hmt, u, vt, layer, sel, x, gate, rows_per_gate, final_gain=None, *, tm, eg):
    t, d = x.shape
    ne = u.shape[1]
    tm = min(tm, t, rows_per_gate)
    te = 2 * eg
    assert rows_per_gate % tm == 0 and t % tm == 0 and ne % te == 0 and eg % PEER_NKEYS == 0
    ni = te // PEER_NKEYS
    assert ni % SUBLANES == 0
    s1, e1, dd, e0 = sel
    gpb = rows_per_gate // tm
    last = ne // eg - 1
    once = pl.Buffered(1)
    full = pl.BlockSpec((PEER_HEADS, PEER_NKEYS, tm), lambda i, e: (0, 0, i))
    rowsp = pl.BlockSpec((PEER_HEADS, ni, tm), lambda i, e: (0, e, i))
    in_specs = [
        pl.BlockSpec((d, tm), lambda i, e: (0, i)),
        pl.BlockSpec((None, eg, d), lambda i, e: (layer, 0, 0), pipeline_mode=once),
        pl.BlockSpec((None, eg, d), lambda i, e: (layer, 2 * e + 1, 0)),
        pl.BlockSpec((None, eg, d), lambda i, e: (layer, jnp.minimum(2 * e + 2, last), 0)),
        pl.BlockSpec((None, d, te), lambda i, e: (layer, 0, e)),
        full, full, rowsp, rowsp,
        pl.BlockSpec((tm, d), lambda i, e: (i, 0), pipeline_mode=once),
        pl.BlockSpec((1, 1, d), lambda i, e: (i // gpb, 0, 0)),
    ]
    args = [hmt, u, u, u, vt, s1, e1, dd, e0, x, gate.reshape(gate.shape[0], 1, d)]
    if final_gain is not None:
        in_specs.append(pl.BlockSpec((1, d), lambda i, e: (0, 0)))
        args.append(final_gain.reshape(1, d))
    return pl.pallas_call(
        functools.partial(_peer_dense_kernel, final_norm=final_gain is not None),
        grid=(t // tm, ne // te),
        in_specs=in_specs,
        out_specs=pl.BlockSpec((tm, d), lambda i, e: (i, 0)),
        out_shape=jax.ShapeDtypeStruct((t, d), jnp.float32),
        scratch_shapes=[pltpu.VMEM((d, tm), jnp.float32), pltpu.VMEM((te, tm), MXU_DTYPE),
                        pltpu.VMEM((eg, tm), jnp.float32)],
        compiler_params=_cparams(("arbitrary", "arbitrary"), 60),
        name="peer_dense",
    )(*args)


def peer_block(x, gain, shift, scale, gate, w_q, keys, u, vt, layer, final_gain=None):
    b, l, d = x.shape
    q, hmt = mod_matmul(x, gain, shift, scale, w_q, tm=1024, tn=512, emit_a=True)
    sel = peer_topk(q.reshape(b * l, d), keys, tt=256)
    out = peer_dense(hmt, u, vt, layer, sel, x.reshape(b * l, d), gate, l,
                     final_gain, tm=256, eg=512)
    return out.reshape(b, l, d)


def _pack_even_w_in(w_in):
    d = w_in.shape[0]
    o = 4 * GDN_W
    h = GDN_HEADS
    ba = w_in[:, o:o + 4 * h]
    zeros = jnp.zeros((d, LANES - 2 * h), w_in.dtype)
    fwd = jnp.concatenate([ba[:, 0:h], ba[:, 2 * h:3 * h], zeros], axis=1)
    bwd = jnp.concatenate([ba[:, h:2 * h], ba[:, 3 * h:4 * h], zeros], axis=1)
    pad = jnp.zeros((d, 2 * LANES), w_in.dtype)
    packed = jnp.concatenate([w_in[:, :o], w_in[:, o + 4 * h:], fwd, bwd, pad], axis=1)
    return packed.astype(MXU_DTYPE)


def _lane_rows(vals):
    z = jnp.zeros((2, GDN_HEADS), jnp.float32)
    pad = jnp.zeros((2, LANES - 2 * GDN_HEADS), jnp.float32)
    return jnp.concatenate([z, vals.astype(jnp.float32), pad], axis=1).reshape(2, 1, LANES)


def even_layer_mixer(x, h_c, m_lat, m_ctx, gain, w_in, conv_w, a_log, dt_bias, out_norm, w_out,
                     ctx_out):
    b = x.shape[0]
    w_packed = _pack_even_w_in(w_in)
    ba_tile0 = (4 * GDN_W + FNET_W) // LANES
    z_tile = 3
    f_tile = 4 * GDN_W // FNET_W
    a_rows = _lane_rows(jnp.exp(a_log.astype(jnp.float32)))
    dt_rows = _lane_rows(dt_bias)
    w_out_c = w_out.astype(MXU_DTYPE)
    w_y, w_f = w_out_c[:GDN_W], w_out_c[GDN_W:]

    p_ctx = mod_matmul(h_c, gain, m_ctx[0], m_ctx[1], w_packed, tm=512, tn=512)
    p_lat = mod_matmul(x, gain, m_lat[0], m_lat[1], w_packed, tm=1024, tn=512)
    zero = jnp.zeros((2, b, GDN_HEADS, GDN_DK, GDN_DV), jnp.float32)
    o_ctx, s_ctx = gdn_chunks(gdn_short_conv(p_ctx, conv_w), p_ctx, ba_tile0, a_rows, dt_rows, zero,
                              nb=2)
    o_lat, _ = gdn_chunks(gdn_short_conv(p_lat, conv_w), p_lat, ba_tile0, a_rows, dt_rows, s_ctx,
                          nb=2)

    def finish(p, o, resid, gate):
        y = gated_out(o, p, z_tile, out_norm, tm=512)
        f = fourier_mix(p, f_tile, tm=512)
        return resid_matmul([y, f], [w_y, w_f], resid, gate, tm=1024, tn=512)

    x = finish(p_lat, o_lat, x, m_lat[2])
    if ctx_out:
        h_c = finish(p_ctx, o_ctx, h_c, m_ctx[2])
    return x, h_c


def odd_layer_mixer(x, h_c, m_lat, m_ctx, gain, w_qkv, sinks, w_out, ctx_out):
    l = x.shape[1]
    w_c = w_qkv.astype(MXU_DTYPE)
    w_out_c = w_out.astype(MXU_DTYPE)
    rope = _rope_tables(l)
    tn = 256
    qkv = mod_matmul(x, gain, m_lat[0], m_lat[1], w_c, tm=1024, tn=tn,
                     rope=rope, rope_tiles=(SWA_Q_W + SWA_KV_W) // tn)
    ckv = mod_matmul(h_c, gain, m_ctx[0], m_ctx[1], w_c[:, SWA_Q_W:], tm=512, tn=tn)
    sinks = sinks.astype(jnp.float32)
    o = swa_attention(qkv, ckv, sinks, use_window=True)
    x_new = resid_matmul([o], [w_out_c], x, m_lat[2], tm=1024, tn=512)
    if ctx_out:
        q_c = mod_matmul(h_c, gain, m_ctx[0], m_ctx[1], w_c[:, :SWA_Q_W], tm=512, tn=tn)
        o_c = swa_attention(q_c, ckv, sinks, use_window=False)
        h_c = resid_matmul([o_c], [w_out_c], h_c, m_ctx[2], tm=1024, tn=512)
    return x_new, h_c


def kernel(x, c, ctx, c_ctx, ada_w, ada_b, norm_mix, norm_ffn, even_w_in, gdn_conv, gdn_a_log, gdn_dt_bias, gdn_out_norm, even_w_out, odd_w_qkv, odd_sinks, odd_w_out, peer_w_q, peer_keys, peer_u, peer_v, final_norm):
    b, _, d = x.shape
    depth = ada_w.shape[0]
    rows = 16
    cond = jnp.concatenate([c, c_ctx[None, :], jnp.zeros((rows - b - 1, d), c.dtype)], axis=0)
    mods = ada_mod_all(cond, ada_w, ada_b)
    conv_w_all = gdn_conv
    u_all = peer_u.astype(MXU_DTYPE)
    vt_all = jnp.swapaxes(peer_v.astype(MXU_DTYPE), 1, 2)
    h_c = ctx
    for i in range(depth):
        last = i == depth - 1
        j = i // 2
        m = mods[i].reshape(rows, N_MOD, d)
        m_lat = [m[:b, k] for k in range(N_MOD)]
        m_ctx = [jnp.broadcast_to(m[b, k][None, :], (b, d)) for k in range(N_MOD)]
        if i % 2 == 0:
            x, h_c = even_layer_mixer(x, h_c, m_lat, m_ctx, norm_mix[i], even_w_in[j], conv_w_all[j],
                                      gdn_a_log[j], gdn_dt_bias[j], gdn_out_norm[j], even_w_out[j],
                                      not last)
        else:
            x, h_c = odd_layer_mixer(x, h_c, m_lat, m_ctx, norm_mix[i], odd_w_qkv[j], odd_sinks[j],
                                     odd_w_out[j], not last)
        w_q = peer_w_q[i].astype(MXU_DTYPE)
        keys = peer_keys[i].astype(jnp.float32)
        x = peer_block(x, norm_ffn[i], m_lat[3], m_lat[4], m_lat[5], w_q, keys, u_all, vt_all, i,
                       final_norm if last else None)
        if not last:
            h_c = peer_block(h_c, norm_ffn[i], m_ctx[3], m_ctx[4], m_ctx[5], w_q, keys,
                             u_all, vt_all, i)
    return x
```

```python
import functools
import math

import jax
import jax.numpy as jnp
from jax import lax
from jax.experimental import pallas as pl
from jax.experimental.pallas import tpu as pltpu

N_MOD = 6
EPS = 1e-6
GDN_HEADS = 12
GDN_DK = 128
GDN_DV = 128
GDN_W = GDN_HEADS * GDN_DV
GDN_CONV = 5
GDN_CHUNK = 64
GDN_PACK = 2
GDN_CONV_HEADS = 4
FNET_GROUPS = 4
FNET_GW = 128
FNET_W = FNET_GROUPS * FNET_GW
GRID_W = 64
SWA_HEADS = 32
SWA_KV = 4
SWA_GROUP = SWA_HEADS // SWA_KV
SWA_DH = 64
SWA_WINDOW = 128
SWA_BLOCK = 128
ROPE_BASE = 10000.0
SWA_Q_W = SWA_HEADS * SWA_DH
SWA_KV_W = SWA_KV * SWA_DH
PEER_HEADS = 8
PEER_NKEYS = 128
PEER_HALF = 128
PEER_TOPK = 16

LANES = 128
SUBLANES = 8
VMEM_BYTES_V7X = 64 * 1024 * 1024
MXU_DTYPE = jnp.bfloat16

_HI = lax.Precision.HIGHEST
_NT = (((1,), (1,)), ((), ()))
_TN = (((0,), (0,)), ((), ()))


def _cparams(semantics, vmem_mb):
    assert vmem_mb * 1024 * 1024 < VMEM_BYTES_V7X
    return pltpu.CompilerParams(dimension_semantics=semantics,
                                vmem_limit_bytes=vmem_mb * 1024 * 1024)


def _mm(a, b):
    return jnp.dot(a.astype(MXU_DTYPE), b.astype(MXU_DTYPE),
                   preferred_element_type=jnp.float32)


def _mm_nt(a, b):
    return lax.dot_general(a.astype(MXU_DTYPE), b.astype(MXU_DTYPE), _NT,
                           preferred_element_type=jnp.float32)


def _mm_tn(a, b):
    return lax.dot_general(a.astype(MXU_DTYPE), b.astype(MXU_DTYPE), _TN,
                           preferred_element_type=jnp.float32)


def _mm_f32(a, b):
    return jnp.dot(a, b, precision=_HI, preferred_element_type=jnp.float32)


def _split(a):
    hi = a.astype(MXU_DTYPE)
    lo = (a - hi.astype(jnp.float32)).astype(MXU_DTYPE)
    return hi, lo


def _mm_x3(a, b):
    (ah, al), (bh, bl) = a, b
    dot = functools.partial(jnp.dot, preferred_element_type=jnp.float32)
    m = ah.shape[0]
    both = dot(jnp.concatenate([ah, al], axis=0), bh)
    return both[:m] + (dot(ah, bl) + both[m:])


def _sigmoid(x):
    return 1.0 / (1.0 + jnp.exp(-x))


def _silu(x):
    return x * _sigmoid(x)


def _softplus(x):
    return jnp.maximum(x, 0.0) + jnp.log1p(jnp.exp(-jnp.abs(x)))


def _ada_kernel(cond_ref, w_ref, b_ref, o_ref):
    a = _silu(cond_ref[...])
    o_ref[0] = _mm(a, w_ref[0]) + b_ref[0]


def ada_mod_all(cond, ada_w, ada_b, *, tn=1024):
    depth, d, n = ada_w.shape
    r = cond.shape[0]
    return pl.pallas_call(
        _ada_kernel,
        grid=(depth, n // tn),
        in_specs=[
            pl.BlockSpec((r, d), lambda l, j: (0, 0)),
            pl.BlockSpec((1, d, tn), lambda l, j: (l, 0, j)),
            pl.BlockSpec((1, 1, tn), lambda l, j: (l, 0, j)),
        ],
        out_specs=pl.BlockSpec((1, r, tn), lambda l, j: (l, 0, j)),
        out_shape=jax.ShapeDtypeStruct((depth, r, n), jnp.float32),
        compiler_params=_cparams(("parallel", "parallel"), 40),
        name="ada_mod",
    )(cond, ada_w, ada_b.reshape(depth, 1, n))


def _rope_swap(y):
    n = y.shape[-1]
    lane = lax.broadcasted_iota(jnp.int32, y.shape, 1)
    up = pltpu.roll(y, n - 16, 1)
    down = pltpu.roll(y, 16, 1)
    return jnp.where(lane % 32 < 16, up, down)


def _mod_matmul_kernel(*refs, rope_tiles, emit_a, tn):
    x_ref, gain_ref, shift_ref, scale_ref, w_ref = refs[:5]
    pos = 5
    if rope_tiles:
        cos_ref, sin_ref = refs[pos:pos + 2]
        pos += 2
    o_ref = refs[pos]
    pos += 1
    if emit_a:
        a_out_ref = refs[pos]
        pos += 1
    a_ref = refs[pos]
    j = pl.program_id(2)

    @pl.when(j == 0)
    def _():
        x = x_ref[0]
        y = x * lax.rsqrt(jnp.mean(x * x, axis=-1, keepdims=True) + EPS)
        y = y * gain_ref[...]
        a = y * (1.0 + scale_ref[0]) + shift_ref[0]
        a_ref[...] = a.astype(a_ref.dtype)
        if emit_a:
            for k in range(a_out_ref.shape[0]):
                a_out_ref[k] = a[k * emit_a:(k + 1) * emit_a].T.astype(a_out_ref.dtype)

    acc = jnp.dot(a_ref[...], w_ref[...], preferred_element_type=jnp.float32)
    if rope_tiles:
        @pl.when(j < rope_tiles)
        def _():
            cos = cos_ref[...]
            sin = sin_ref[...]
            for t in range(tn // LANES):
                sl = slice(t * LANES, (t + 1) * LANES)
                y = acc[:, sl]
                o_ref[0, :, sl] = (y * cos + _rope_swap(y) * sin).astype(o_ref.dtype)

        @pl.when(j >= rope_tiles)
        def _():
            o_ref[0] = acc.astype(o_ref.dtype)
    else:
        o_ref[0] = acc.astype(o_ref.dtype)


def mod_matmul(x, gain, shift, scale, w, *, tm, tn, rope=None, rope_tiles=0, emit_a=False):
    b, l, d = x.shape
    n = w.shape[1]
    tm = min(tm, l)
    assert l % tm == 0 and n % tn == 0 and tn % LANES == 0
    in_specs = [
        pl.BlockSpec((1, tm, d), lambda bi, i, j: (bi, i, 0)),
        pl.BlockSpec((1, d), lambda bi, i, j: (0, 0)),
        pl.BlockSpec((1, 1, d), lambda bi, i, j: (bi, 0, 0)),
        pl.BlockSpec((1, 1, d), lambda bi, i, j: (bi, 0, 0)),
        pl.BlockSpec((d, tn), lambda bi, i, j: (0, j)),
    ]
    args = [x, gain.reshape(1, d), shift.reshape(b, 1, d), scale.reshape(b, 1, d), w]
    if rope_tiles:
        in_specs += [pl.BlockSpec((tm, LANES), lambda bi, i, j: (i, 0))] * 2
        args += list(rope)
    out_specs = [pl.BlockSpec((1, tm, tn), lambda bi, i, j: (bi, i, j))]
    out_shape = [jax.ShapeDtypeStruct((b, l, n), jnp.float32)]
    if emit_a:
        nb = l // tm
        assert tm % emit_a == 0
        ns = tm // emit_a
        out_specs.append(pl.BlockSpec((ns, d, emit_a), lambda bi, i, j: (bi * nb + i, 0, 0)))
        out_shape.append(jax.ShapeDtypeStruct((b * l // emit_a, d, emit_a), w.dtype))
    res = pl.pallas_call(
        functools.partial(_mod_matmul_kernel, rope_tiles=rope_tiles, emit_a=emit_a, tn=tn),
        grid=(b, l // tm, n // tn),
        in_specs=in_specs,
        out_specs=out_specs,
        out_shape=out_shape,
        scratch_shapes=[pltpu.VMEM((tm, d), w.dtype)],
        compiler_params=_cparams(("parallel", "parallel", "arbitrary"), 56),
        name="mod_matmul",
    )(*args)
    return res if emit_a else res[0]


def _resid_matmul_kernel(*refs, n_pairs):
    a_refs = refs[:n_pairs]
    w_refs = refs[n_pairs:2 * n_pairs]
    x_ref, gate_ref, o_ref = refs[2 * n_pairs:]
    acc = jnp.dot(a_refs[0][0], w_refs[0][...], preferred_element_type=jnp.float32)
    for a_ref, w_ref in zip(a_refs[1:], w_refs[1:]):
        acc += jnp.dot(a_ref[0], w_ref[...], preferred_element_type=jnp.float32)
    o_ref[0] = x_ref[0] + gate_ref[0] * acc


def resid_matmul(acts, weights, x, gate, *, tm, tn):
    b, l, n = x.shape
    tm = min(tm, l)
    in_specs = []
    for a in acts:
        in_specs.append(pl.BlockSpec((1, tm, a.shape[2]), lambda bi, i, j: (bi, i, 0)))
    for w in weights:
        in_specs.append(pl.BlockSpec((w.shape[0], tn), lambda bi, i, j: (0, j)))
    in_specs += [
        pl.BlockSpec((1, tm, tn), lambda bi, i, j: (bi, i, j)),
        pl.BlockSpec((1, 1, tn), lambda bi, i, j: (bi, 0, j)),
    ]
    return pl.pallas_call(
        functools.partial(_resid_matmul_kernel, n_pairs=len(acts)),
        grid=(b, l // tm, n // tn),
        in_specs=in_specs,
        out_specs=pl.BlockSpec((1, tm, tn), lambda bi, i, j: (bi, i, j)),
        out_shape=jax.ShapeDtypeStruct((b, l, n), jnp.float32),
        compiler_params=_cparams(("parallel", "parallel", "parallel"), 40),
        name="resid_matmul",
    )(*acts, *weights, x, gate.reshape(b, 1, n))


def _gdn_conv_kernel(p_ref, w_ref, o_ref):
    c = pl.program_id(1)
    l = p_ref.shape[1]
    row = lax.broadcasted_iota(jnp.int32, (l, GDN_DK), 0)
    pad = (GDN_CONV - 1) // 2
    for hh in range(GDN_CONV_HEADS):
        sl = slice(hh * GDN_DK, (hh + 1) * GDN_DK)
        head = c * GDN_CONV_HEADS + hh
        x = p_ref[0, :, sl]
        y = x * w_ref[0, pad:pad + 1, sl]
        for t in range(GDN_CONV):
            s = t - pad
            if s == 0:
                continue
            xs = pltpu.roll(x, (-s) % l, 0)
            ok = (row + s >= 0) & (row + s < l)
            y = y + jnp.where(ok, xs, 0.0) * w_ref[0, t:t + 1, sl]
        y = _silu(y)
        inv = lax.rsqrt(jnp.sum(y * y, axis=-1, keepdims=True) + 1e-6)
        fac = jnp.where(head < 2 * GDN_HEADS, inv, 1.0)
        fac = fac * jnp.where(head < GDN_HEADS, GDN_DK ** -0.5, 1.0)
        o_ref[0, :, sl] = y * fac


def gdn_short_conv(p, conv_w):
    b, l, _ = p.shape
    tw = GDN_CONV_HEADS * GDN_DK
    nt = 3 * GDN_W // tw
    w = conv_w.reshape(GDN_CONV, nt, tw).transpose(1, 0, 2)
    return pl.pallas_call(
        _gdn_conv_kernel,
        grid=(b, nt),
        in_specs=[
            pl.BlockSpec((1, l, tw), lambda bi, c: (bi, 0, c)),
            pl.BlockSpec((1, GDN_CONV, tw), lambda bi, c: (c, 0, 0)),
        ],
        out_specs=pl.BlockSpec((1, l, tw), lambda bi, c: (bi, 0, c)),
        out_shape=jax.ShapeDtypeStruct((b, l, 3 * GDN_W), jnp.float32),
        compiler_params=_cparams(("parallel", "parallel"), 40),
        name="gdn_conv",
    )(p, w)


def _lane_blocks(cat, n):
    shift = int(math.log2(cat.shape[1] // n))
    blk = jnp.right_shift(lax.broadcasted_iota(jnp.int32, cat.shape, 1), shift)
    return [jnp.where(blk == b, cat, 0.0) for b in range(n)]


def _block_diag_split(cat, n):
    hi = cat.astype(MXU_DTYPE).astype(jnp.float32)
    lo = cat - hi
    return tuple(jnp.concatenate(_lane_blocks(x, n), axis=0).astype(MXU_DTYPE) for x in (hi, lo))


def _unit_lower_inverse(lms, n):
    c = lms[0].shape[0]
    row = lax.broadcasted_iota(jnp.int32, (c, n * c), 0)
    lane = lax.broadcasted_iota(jnp.int32, (c, n * c), 1)
    eye = (jnp.bitwise_and(lane, c - 1) == row).astype(jnp.float32)
    ps = [-jnp.concatenate(lms[g:g + n], axis=1) for g in range(0, len(lms), n)]
    ts = [eye + p for p in ps]
    pbds = [_block_diag_split(p, n) for p in ps]
    for _ in range(int(math.log2(c)) - 1):
        ps = [_mm_x3(_split(p), pbd) for p, pbd in zip(ps, pbds)]
        pbds = [_block_diag_split(p, n) for p in ps]
        ts = [t + _mm_x3(_split(t), pbd) for t, pbd in zip(ts, pbds)]
    return ts


def _gdn_chunk_kernel(q_ref, k_ref, v_ref, ba_ref, a_ref, dt_ref, s0_ref,
                      o_ref, sout_ref, s_ref):
    d = pl.program_id(0)
    n = pl.program_id(2)
    c = GDN_CHUNK
    nb = q_ref.shape[0]

    @pl.when(n == 0)
    def _():
        for bb in range(nb):
            s_ref[bb * GDN_HEADS:(bb + 1) * GDN_HEADS] = s0_ref[0, bb]

    ii = lax.broadcasted_iota(jnp.int32, (c, c), 0)
    jj = lax.broadcasted_iota(jnp.int32, (c, c), 1)
    sgn = 1 - 2 * d
    diff = (ii - jj) * sgn
    incl = diff >= 0
    strict = diff > 0
    tri = incl.astype(jnp.float32)

    beta_alls, gc_alls, gc_all_ts, eg_alls, er_alls, el_alls = [], [], [], [], [], []
    for bb in range(nb):
        ba = ba_ref[bb]
        g_all = -a_ref[0] * _softplus(ba + dt_ref[0])
        gc_all = _mm_f32(tri, g_all)
        gtot_all = jnp.sum(g_all, axis=0, keepdims=True)
        beta_alls.append(_sigmoid(ba))
        gc_alls.append(gc_all)
        gc_all_ts.append(gc_all.T)
        eg_alls.append(jnp.exp(gc_all))
        er_alls.append(jnp.exp(gtot_all - gc_all))
        el_alls.append(jnp.exp(gtot_all))

    units = [(bb, h) for bb in range(nb) for h in range(GDN_HEADS)]

    def col(tables, bb, lane):
        return tables[bb][:, lane:lane + 1]

    sls = [slice(h * GDN_DK, (h + 1) * GDN_DK) for _, h in units]
    ks = [k_ref[bb, :, sl] for (bb, _), sl in zip(units, sls)]
    kbs = [k * col(beta_alls, bb, h) for (bb, h), k in zip(units, ks)]
    decs = [jnp.exp(jnp.where(incl, col(gc_alls, bb, GDN_HEADS + h)
                              - gc_all_ts[bb][GDN_HEADS + h:GDN_HEADS + h + 1, :], -jnp.inf))
            for bb, h in units]
    lms = [jnp.where(strict, _mm_nt(kb, k) * dec, 0.0) for kb, k, dec in zip(kbs, ks, decs)]
    us, ws = [], []
    tinvs = _unit_lower_inverse(lms, GDN_PACK)
    for g0, tinv in zip(range(0, len(units), GDN_PACK), tinvs):
        grp = range(g0, g0 + GDN_PACK)
        vb = jnp.concatenate([v_ref[units[x][0], :, sls[x]] * col(beta_alls, *units[x])
                              for x in grp], axis=0)
        kg = jnp.concatenate([kbs[x] * col(eg_alls, units[x][0], GDN_HEADS + units[x][1])
                              for x in grp], axis=0)
        for t_x in _lane_blocks(tinv, GDN_PACK):
            us.append(_mm(t_x, vb))
            ws.append(_mm(t_x, kg))
    qs = [q_ref[bb, :, sl] for (bb, _), sl in zip(units, sls)]
    attns = [jnp.where(incl, _mm_nt(q, k) * dec, 0.0) for q, k, dec in zip(qs, ks, decs)]
    ss = [s_ref[bb * GDN_HEADS + h] for bb, h in units]
    v_news = [u - _mm(w, s) for u, w, s in zip(us, ws, ss)]
    for x, (bb, h) in enumerate(units):
        qg = qs[x] * col(eg_alls, bb, GDN_HEADS + h)
        o_ref[0, bb, :, sls[x]] = _mm(qg, ss[x]) + _mm(attns[x], v_news[x])
    for x, (bb, h) in enumerate(units):
        kd = ks[x] * col(er_alls, bb, GDN_HEADS + h)
        s_ref[bb * GDN_HEADS + h] = (ss[x] * col(el_alls, bb, GDN_HEADS + h)
                                     + _mm_tn(kd, v_news[x]))

    @pl.when(n == pl.num_programs(2) - 1)
    def _():
        for bb in range(nb):
            sout_ref[0, bb] = s_ref[bb * GDN_HEADS:(bb + 1) * GDN_HEADS]


def gdn_chunks(qkv, p, ba_tile0, a_rows, dt_rows, s0, *, nb):
    b, l, _ = qkv.shape
    nc = l // GDN_CHUNK
    nb = math.gcd(nb, b)

    def row(d, bi, n):
        return n + d * (nc - 1 - 2 * n)

    return pl.pallas_call(
        _gdn_chunk_kernel,
        grid=(2, b // nb, nc),
        in_specs=[
            pl.BlockSpec((nb, GDN_CHUNK, GDN_W), lambda d, bi, n: (bi, row(d, bi, n), 0)),
            pl.BlockSpec((nb, GDN_CHUNK, GDN_W), lambda d, bi, n: (bi, row(d, bi, n), 1)),
            pl.BlockSpec((nb, GDN_CHUNK, GDN_W), lambda d, bi, n: (bi, row(d, bi, n), 2)),
            pl.BlockSpec((nb, GDN_CHUNK, LANES), lambda d, bi, n: (bi, row(d, bi, n), ba_tile0 + d)),
            pl.BlockSpec((1, 1, LANES), lambda d, bi, n: (d, 0, 0)),
            pl.BlockSpec((1, 1, LANES), lambda d, bi, n: (d, 0, 0)),
            pl.BlockSpec((1, nb, GDN_HEADS, GDN_DK, GDN_DV), lambda d, bi, n: (d, bi, 0, 0, 0)),
        ],
        out_specs=[
            pl.BlockSpec((1, nb, GDN_CHUNK, GDN_W), lambda d, bi, n: (d, bi, row(d, bi, n), 0)),
            pl.BlockSpec((1, nb, GDN_HEADS, GDN_DK, GDN_DV), lambda d, bi, n: (d, bi, 0, 0, 0)),
        ],
        out_shape=[
            jax.ShapeDtypeStruct((2, b, l, GDN_W), jnp.float32),
            jax.ShapeDtypeStruct((2, b, GDN_HEADS, GDN_DK, GDN_DV), jnp.float32),
        ],
        scratch_shapes=[pltpu.VMEM((nb * GDN_HEADS, GDN_DK, GDN_DV), jnp.float32)],
        compiler_params=_cparams(("parallel", "parallel", "arbitrary"), 40),
        name="gdn_chunks",
    )(qkv, qkv, qkv, p, a_rows, dt_rows, s0)


def _gated_out_kernel(of_ref, ob_ref, z_ref, w_ref, y_ref):
    for h in range(GDN_HEADS):
        sl = slice(h * GDN_DV, (h + 1) * GDN_DV)
        o = of_ref[0, 0, :, sl] + ob_ref[0, 0, :, sl]
        y = o * lax.rsqrt(jnp.mean(o * o, axis=-1, keepdims=True) + EPS)
        y_ref[0, :, sl] = (y * w_ref[...] * _silu(z_ref[0, :, sl])).astype(y_ref.dtype)


def gated_out(o, p, z_tile, w_norm, *, tm):
    _, b, l, _ = o.shape
    tm = min(tm, l)
    return pl.pallas_call(
        _gated_out_kernel,
        grid=(b, l // tm),
        in_specs=[
            pl.BlockSpec((1, 1, tm, GDN_W), lambda bi, i: (0, bi, i, 0)),
            pl.BlockSpec((1, 1, tm, GDN_W), lambda bi, i: (1, bi, i, 0)),
            pl.BlockSpec((1, tm, GDN_W), lambda bi, i: (bi, i, z_tile)),
            pl.BlockSpec((1, GDN_DV), lambda bi, i: (0, 0)),
        ],
        out_specs=pl.BlockSpec((1, tm, GDN_W), lambda bi, i: (bi, i, 0)),
        out_shape=jax.ShapeDtypeStruct((b, l, GDN_W), MXU_DTYPE),
        compiler_params=_cparams(("parallel", "parallel"), 40),
        name="gated_out",
    )(o, o, p, w_norm.reshape(1, GDN_DV))


def _fourier_kernel(x_ref, bd_ref, cs_ref, o_ref, z_ref):
    i = pl.program_id(1)
    l = x_ref.shape[1]

    @pl.when(i == 0)
    def _():
        z = _mm(x_ref[0], bd_ref[...])
        z_ref[0:l, :] = z[:, :FNET_W].astype(z_ref.dtype)
        z_ref[l:2 * l, :] = z[:, FNET_W:].astype(z_ref.dtype)

    o_ref[0] = jnp.dot(cs_ref[...], z_ref[...],
                       preferred_element_type=jnp.float32).astype(o_ref.dtype)


def _dft_tables(l):
    def cs(n, scale):
        k = jnp.arange(n, dtype=jnp.int32)
        ang = ((k[:, None] * k[None, :]) % n).astype(jnp.float32) * (2.0 * math.pi / n)
        return jnp.cos(ang) * scale, jnp.sin(ang) * scale

    c_l, s_l = cs(l, 1.0)
    c_g, s_g = cs(FNET_GW, 1.0 / math.sqrt(l * FNET_GW))
    eye = jnp.eye(FNET_GROUPS, dtype=jnp.float32)
    bd = jnp.concatenate([jnp.kron(eye, c_g), jnp.kron(eye, s_g)], axis=1)
    cs_l = jnp.concatenate([c_l, -s_l], axis=1)
    return bd.astype(MXU_DTYPE), cs_l.astype(MXU_DTYPE)


def fourier_mix(p, f_tile, *, tm):
    b, l, _ = p.shape
    tm = min(tm, l)
    bd, cs_l = _dft_tables(l)
    return pl.pallas_call(
        _fourier_kernel,
        grid=(b, l // tm),
        in_specs=[
            pl.BlockSpec((1, l, FNET_W), lambda bi, i: (bi, 0, f_tile)),
            pl.BlockSpec((FNET_W, 2 * FNET_W), lambda bi, i: (0, 0)),
            pl.BlockSpec((tm, 2 * l), lambda bi, i: (i, 0)),
        ],
        out_specs=pl.BlockSpec((1, tm, FNET_W), lambda bi, i: (bi, i, 0)),
        out_shape=jax.ShapeDtypeStruct((b, l, FNET_W), MXU_DTYPE),
        scratch_shapes=[pltpu.VMEM((2 * l, FNET_W), MXU_DTYPE)],
        compiler_params=_cparams(("parallel", "arbitrary"), 40),
        name="fourier_mix",
    )(p, bd, cs_l)


def _swa_kernel(*refs, use_window):
    if use_window:
        sink_ref, q_ref, k_ref, v_ref, kc_ref, vc_ref, o_ref = refs
    else:
        sink_ref, q_ref, kc_ref, vc_ref, o_ref = refs
    i = pl.program_id(1)
    tq = q_ref.shape[1]
    scale = SWA_DH ** -0.5
    if use_window:
        l = k_ref.shape[1]
        span = SWA_BLOCK + 2 * SWA_WINDOW
        start = i * tq
        base = pl.multiple_of(jnp.clip(start - SWA_WINDOW, 0, l - span), SWA_BLOCK)
        qpos = start + lax.broadcasted_iota(jnp.int32, (tq, span), 0)
        kpos = base + lax.broadcasted_iota(jnp.int32, (tq, span), 1)
        band = jnp.abs(qpos - kpos) <= SWA_WINDOW
    for g in range(SWA_KV):
        ksl = slice(g * SWA_DH, (g + 1) * SWA_DH)
        qg = jnp.concatenate(
            [q_ref[0, :, (g * SWA_GROUP + hh) * SWA_DH:(g * SWA_GROUP + hh + 1) * SWA_DH]
             for hh in range(SWA_GROUP)], axis=0)
        kc = kc_ref[0, :, ksl]
        vc = vc_ref[0, :, ksl]
        lc_all = _mm_nt(qg, kc) * scale
        if use_window:
            kw = k_ref[0, pl.ds(base, span), ksl]
            vw = v_ref[0, pl.ds(base, span), ksl]
            lw_all = _mm_nt(qg, kw) * scale
        outs = []
        for hh in range(SWA_GROUP):
            rs = slice(hh * tq, (hh + 1) * tq)
            sink = sink_ref[g * SWA_GROUP + hh]
            lc = lc_all[rs]
            m = jnp.maximum(jnp.max(lc, axis=-1, keepdims=True), sink)
            if use_window:
                lw = jnp.where(band, lw_all[rs], -jnp.inf)
                m = jnp.maximum(m, jnp.max(lw, axis=-1, keepdims=True))
                pw = jnp.exp(lw - m)
            pc = jnp.exp(lc - m)
            den = jnp.sum(pc, axis=-1, keepdims=True) + jnp.exp(sink - m)
            acc = _mm(pc, vc)
            if use_window:
                den = den + jnp.sum(pw, axis=-1, keepdims=True)
                acc = acc + _mm(pw, vw)
            outs.append(acc / den)
        for pair in range(SWA_GROUP // 2):
            lo = (g * SWA_GROUP + 2 * pair) * SWA_DH
            o_ref[0, :, lo:lo + 2 * SWA_DH] = jnp.concatenate(
                [outs[2 * pair], outs[2 * pair + 1]], axis=1).astype(o_ref.dtype)


def swa_attention(qkv, ckv, sinks, *, use_window):
    b, l, _ = qkv.shape
    c = ckv.shape[1]
    tq = SWA_BLOCK
    kt = SWA_Q_W // SWA_KV_W
    in_specs = [
        pl.BlockSpec(memory_space=pltpu.SMEM),
        pl.BlockSpec((1, tq, SWA_Q_W), lambda bi, i: (bi, i, 0)),
    ]
    args = [sinks, qkv]
    if use_window:
        in_specs += [
            pl.BlockSpec((1, l, SWA_KV_W), lambda bi, i: (bi, 0, kt)),
            pl.BlockSpec((1, l, SWA_KV_W), lambda bi, i: (bi, 0, kt + 1)),
        ]
        args += [qkv, qkv]
    in_specs += [
        pl.BlockSpec((1, c, SWA_KV_W), lambda bi, i: (bi, 0, 0)),
        pl.BlockSpec((1, c, SWA_KV_W), lambda bi, i: (bi, 0, 1)),
    ]
    args += [ckv, ckv]
    return pl.pallas_call(
        functools.partial(_swa_kernel, use_window=use_window),
        grid=(b, l // tq),
        in_specs=in_specs,
        out_specs=pl.BlockSpec((1, tq, SWA_Q_W), lambda bi, i: (bi, i, 0)),
        out_shape=jax.ShapeDtypeStruct((b, l, SWA_Q_W), MXU_DTYPE),
        compiler_params=_cparams(("parallel", "parallel"), 40),
        name="swa_attention",
    )(*args)


def _rope_tables(l):
    rows = l // GRID_W
    r = jnp.broadcast_to(jnp.arange(rows)[:, None], (rows, GRID_W)).reshape(-1).astype(jnp.float32)
    col = jnp.broadcast_to(jnp.arange(GRID_W)[None, :], (rows, GRID_W)).reshape(-1).astype(jnp.float32)
    n = SWA_DH // 4
    freq = ROPE_BASE ** (-jnp.arange(n, dtype=jnp.float32) / n)
    ang_r = r[:, None] * freq
    ang_c = col[:, None] * freq
    cos = jnp.concatenate([jnp.cos(ang_r)] * 2 + [jnp.cos(ang_c)] * 2, axis=-1)
    sin = jnp.concatenate([-jnp.sin(ang_r), jnp.sin(ang_r), -jnp.sin(ang_c), jnp.sin(ang_c)], axis=-1)
    reps = LANES // SWA_DH
    return jnp.tile(cos, (1, reps)), jnp.tile(sin, (1, reps))


def _sort_desc(rows):
    n = len(rows)
    assert n & (n - 1) == 0
    rows = list(rows)
    k = 2
    while k <= n:
        j = k // 2
        while j >= 1:
            for i in range(n):
                p = i ^ j
                if p > i:
                    hi, lo = jnp.maximum(rows[i], rows[p]), jnp.minimum(rows[i], rows[p])
                    rows[i], rows[p] = (hi, lo) if (i & k) == 0 else (lo, hi)
            j //= 2
        k *= 2
    return rows


def _top_values(s, count):
    levels = _sort_desc([s[SUBLANES * l:SUBLANES * (l + 1)] for l in range(s.shape[0] // SUBLANES)])
    levels = levels[:count]
    vals = []
    for k in range(count):
        m = jnp.max(levels[0], axis=0, keepdims=True)
        vals.append(m)
        hit = levels[0] == m
        keep = max(count - k - 1, 1)
        below = levels[1:] + [jnp.full_like(levels[0], -jnp.inf)]
        levels = [jnp.where(hit, b, a) for a, b in zip(levels[:keep], below[:keep])]
    return vals


def _peer_cells():
    n = PEER_TOPK + 1
    return [(a, b) for a in range(n) for b in range(n) if (a + 1) * (b + 1) <= n]


def _peer_topk_kernel(q_ref, keys_ref, s1_ref, e1_ref, d_ref, e0_ref):
    n = PEER_TOPK + 1
    cells = _peer_cells()
    for h in range(PEER_HEADS):
        q0 = q_ref[:, (2 * h) * PEER_HALF:(2 * h + 1) * PEER_HALF]
        q1 = q_ref[:, (2 * h + 1) * PEER_HALF:(2 * h + 2) * PEER_HALF]
        s0 = lax.dot_general(keys_ref[0], q0, _NT, precision=_HI,
                             preferred_element_type=jnp.float32)
        s1 = lax.dot_general(keys_ref[1], q1, _NT, precision=_HI,
                             preferred_element_type=jnp.float32)
        top0 = _top_values(s0, n)
        top1 = _top_values(s1, n)
        rows = [top0[a] + top1[b] for a, b in cells]
        rows += [jnp.full_like(rows[0], -jnp.inf)] * (-len(rows) % (8 * SUBLANES))
        best = _top_values(jnp.concatenate(rows, axis=0), n)
        thr = 0.5 * (best[PEER_TOPK - 1] + best[PEER_TOPK])
        z = jnp.ones_like(thr)
        for kk in range(1, PEER_TOPK):
            z = z + jnp.exp(best[kk] - best[0])
        s1_ref[h] = s1
        e1_ref[h] = jnp.exp(s1 - top1[0]) / z
        d_ref[h] = thr - s0
        e0_ref[h] = jnp.exp(s0 - top0[0])


def peer_topk(q, keys, *, tt):
    t = q.shape[0]
    tt = min(tt, t)
    out = jax.ShapeDtypeStruct((t // tt, PEER_HEADS, PEER_NKEYS, tt), jnp.float32)
    ospec = pl.BlockSpec((None, PEER_HEADS, PEER_NKEYS, tt), lambda i: (i, 0, 0, 0))
    return pl.pallas_call(
        _peer_topk_kernel,
        grid=(t // tt,),
        in_specs=[
            pl.BlockSpec((tt, q.shape[1]), lambda i: (i, 0)),
            pl.BlockSpec((2, PEER_NKEYS, PEER_HALF), lambda i: (0, 0, 0)),
        ],
        out_specs=[ospec] * 4,
        out_shape=[out] * 4,
        compiler_params=_cparams(("parallel",), 40),
        name="peer_topk",
    )(q, keys)


def _gelu(x):
    return 0.5 * x * (1.0 + lax.erf(x * (2.0 ** -0.5)))


def _peer_dense_kernel(*refs, final_norm):
    (hmt_ref, u0_ref, ua_ref, ub_ref, vt_ref, s1_ref, e1_ref, d_ref, e0_ref, x_ref,
     gate_ref) = refs[:11]
    pos = 11
    if final_norm:
        fn_ref = refs[pos]
        pos += 1
    o_ref, acc_ref, pt_ref, at_ref = refs[pos:pos + 4]
    e = pl.program_id(1)
    eg, tm = at_ref.shape
    ni = eg // PEER_NKEYS

    def first_matmul(u_ref):
        return jnp.dot(u_ref[...], hmt_ref[...], preferred_element_type=jnp.float32)

    @pl.when(e == 0)
    def _():
        acc_ref[...] = jnp.zeros_like(acc_ref)
        at_ref[...] = first_matmul(u0_ref)

    def gate_group(r, a_block):
        for il in range(ni):
            ig = r * ni + il
            rs = slice(ig * PEER_NKEYS, (ig + 1) * PEER_NKEYS)
            for tl in range(tm // LANES):
                cs = slice(tl * LANES, (tl + 1) * LANES)
                g = None
                for h in range(PEER_HEADS):
                    drow = d_ref[h, ig:ig + 1, cs]
                    erow = e0_ref[h, ig:ig + 1, cs]
                    term = jnp.where(s1_ref[h, :, cs] >= drow, e1_ref[h, :, cs] * erow, 0.0)
                    g = term if g is None else g + term
                pt_ref[rs, cs] = (g * _gelu(a_block(il, cs))).astype(pt_ref.dtype)

    def second_matmul(r):
        es = slice(r * eg, (r + 1) * eg)
        acc_ref[...] += jnp.dot(vt_ref[:, es], pt_ref[es, :], preferred_element_type=jnp.float32)

    a1 = first_matmul(ua_ref)
    gate_group(0, lambda il, cs: at_ref[il * PEER_NKEYS:(il + 1) * PEER_NKEYS, cs])
    second_matmul(0)
    at_ref[...] = first_matmul(ub_ref)
    gate_group(1, lambda il, cs: a1[il * PEER_NKEYS:(il + 1) * PEER_NKEYS, cs])
    second_matmul(1)

    @pl.when(e == pl.num_programs(1) - 1)
    def _():
        y = x_ref[...] + gate_ref[0] * acc_ref[...].T
        if final_norm:
            y = y * lax.rsqrt(jnp.mean(y * y, axis=-1, keepdims=True) + EPS) * fn_ref[...]
        o_ref[...] = y


def peer_dense(hmt, u, vt, layer, sel, x, gate, rows_per_gate, final_gain=None, *, tm, eg):
    t, d = x.shape
    ne = u.shape[1]
    tm = min(tm, t, rows_per_gate)
    te = 2 * eg
    assert rows_per_gate % tm == 0 and t % tm == 0 and ne % te == 0 and eg % PEER_NKEYS == 0
    ni = te // PEER_NKEYS
    assert ni % SUBLANES == 0
    s1, e1, dd, e0 = sel
    gpb = rows_per_gate // tm
    last = ne // eg - 1
    once = pl.Buffered(1)
    assert hmt.shape == (t // tm, d, tm) and s1.shape == (t // tm, PEER_HEADS, PEER_NKEYS, tm)
    full = pl.BlockSpec((None, PEER_HEADS, PEER_NKEYS, tm), lambda i, e: (i, 0, 0, 0))
    rowsp = pl.BlockSpec((None, PEER_HEADS, ni, tm), lambda i, e: (i, 0, e, 0))
    in_specs = [
        pl.BlockSpec((None, d, tm), lambda i, e: (i, 0, 0)),
        pl.BlockSpec((None, eg, d), lambda i, e: (layer, 0, 0), pipeline_mode=once),
        pl.BlockSpec((None, eg, d), lambda i, e: (layer, 2 * e + 1, 0)),
        pl.BlockSpec((None, eg, d), lambda i, e: (layer, jnp.minimum(2 * e + 2, last), 0)),
        pl.BlockSpec((None, d, te), lambda i, e: (layer, 0, e)),
        full, full, rowsp, rowsp,
        pl.BlockSpec((tm, d), lambda i, e: (i, 0), pipeline_mode=once),
        pl.BlockSpec((1, 1, d), lambda i, e: (i // gpb, 0, 0)),
    ]
    args = [hmt, u, u, u, vt, s1, e1, dd, e0, x, gate.reshape(gate.shape[0], 1, d)]
    if final_gain is not None:
        in_specs.append(pl.BlockSpec((1, d), lambda i, e: (0, 0)))
        args.append(final_gain.reshape(1, d))
    return pl.pallas_call(
        functools.partial(_peer_dense_kernel, final_norm=final_gain is not None),
        grid=(t // tm, ne // te),
        in_specs=in_specs,
        out_specs=pl.BlockSpec((tm, d), lambda i, e: (i, 0)),
        out_shape=jax.ShapeDtypeStruct((t, d), jnp.float32),
        scratch_shapes=[pltpu.VMEM((d, tm), jnp.float32), pltpu.VMEM((te, tm), MXU_DTYPE),
                        pltpu.VMEM((eg, tm), jnp.float32)],
        compiler_params=_cparams(("arbitrary", "arbitrary"), 60),
        name="peer_dense",
    )(*args)


def peer_block(x, gain, shift, scale, gate, w_q, keys, u, vt, layer, final_gain=None):
    b, l, d = x.shape
    tg = 256
    q, hmt = mod_matmul(x, gain, shift, scale, w_q, tm=1024, tn=512, emit_a=tg)
    sel = peer_topk(q.reshape(b * l, d), keys, tt=tg)
    out = peer_dense(hmt, u, vt, layer, sel, x.reshape(b * l, d), gate, l,
                     final_gain, tm=tg, eg=512)
    return out.reshape(b, l, d)


def _pack_even_w_in(w_in):
    d = w_in.shape[0]
    o = 4 * GDN_W
    h = GDN_HEADS
    ba = w_in[:, o:o + 4 * h]
    zeros = jnp.zeros((d, LANES - 2 * h), w_in.dtype)
    fwd = jnp.concatenate([ba[:, 0:h], ba[:, 2 * h:3 * h], zeros], axis=1)
    bwd = jnp.concatenate([ba[:, h:2 * h], ba[:, 3 * h:4 * h], zeros], axis=1)
    pad = jnp.zeros((d, 2 * LANES), w_in.dtype)
    packed = jnp.concatenate([w_in[:, :o], w_in[:, o + 4 * h:], fwd, bwd, pad], axis=1)
    return packed.astype(MXU_DTYPE)


def _lane_rows(vals):
    z = jnp.zeros((2, GDN_HEADS), jnp.float32)
    pad = jnp.zeros((2, LANES - 2 * GDN_HEADS), jnp.float32)
    return jnp.concatenate([z, vals.astype(jnp.float32), pad], axis=1).reshape(2, 1, LANES)


def even_layer_mixer(x, h_c, m_lat, m_ctx, gain, w_in, conv_w, a_log, dt_bias, out_norm, w_out,
                     ctx_out):
    b = x.shape[0]
    w_packed = _pack_even_w_in(w_in)
    ba_tile0 = (4 * GDN_W + FNET_W) // LANES
    z_tile = 3
    f_tile = 4 * GDN_W // FNET_W
    a_rows = _lane_rows(jnp.exp(a_log.astype(jnp.float32)))
    dt_rows = _lane_rows(dt_bias)
    w_out_c = w_out.astype(MXU_DTYPE)
    w_y, w_f = w_out_c[:GDN_W], w_out_c[GDN_W:]

    p_ctx = mod_matmul(h_c, gain, m_ctx[0], m_ctx[1], w_packed, tm=512, tn=512)
    p_lat = mod_matmul(x, gain, m_lat[0], m_lat[1], w_packed, tm=1024, tn=512)
    zero = jnp.zeros((2, b, GDN_HEADS, GDN_DK, GDN_DV), jnp.float32)
    o_ctx, s_ctx = gdn_chunks(gdn_short_conv(p_ctx, conv_w), p_ctx, ba_tile0, a_rows, dt_rows, zero,
                              nb=2)
    o_lat, _ = gdn_chunks(gdn_short_conv(p_lat, conv_w), p_lat, ba_tile0, a_rows, dt_rows, s_ctx,
                          nb=2)

    def finish(p, o, resid, gate):
        y = gated_out(o, p, z_tile, out_norm, tm=512)
        f = fourier_mix(p, f_tile, tm=512)
        return resid_matmul([y, f], [w_y, w_f], resid, gate, tm=1024, tn=512)

    x = finish(p_lat, o_lat, x, m_lat[2])
    if ctx_out:
        h_c = finish(p_ctx, o_ctx, h_c, m_ctx[2])
    return x, h_c


def odd_layer_mixer(x, h_c, m_lat, m_ctx, gain, w_qkv, sinks, w_out, ctx_out):
    l = x.shape[1]
    w_c = w_qkv.astype(MXU_DTYPE)
    w_out_c = w_out.astype(MXU_DTYPE)
    rope = _rope_tables(l)
    tn = 256
    qkv = mod_matmul(x, gain, m_lat[0], m_lat[1], w_c, tm=1024, tn=tn,
                     rope=rope, rope_tiles=(SWA_Q_W + SWA_KV_W) // tn)
    ckv = mod_matmul(h_c, gain, m_ctx[0], m_ctx[1], w_c[:, SWA_Q_W:], tm=512, tn=tn)
    sinks = sinks.astype(jnp.float32)
    o = swa_attention(qkv, ckv, sinks, use_window=True)
    x_new = resid_matmul([o], [w_out_c], x, m_lat[2], tm=1024, tn=512)
    if ctx_out:
        q_c = mod_matmul(h_c, gain, m_ctx[0], m_ctx[1], w_c[:, :SWA_Q_W], tm=512, tn=tn)
        o_c = swa_attention(q_c, ckv, sinks, use_window=False)
        h_c = resid_matmul([o_c], [w_out_c], h_c, m_ctx[2], tm=1024, tn=512)
    return x_new, h_c


def kernel(x, c, ctx, c_ctx, ada_w, ada_b, norm_mix, norm_ffn, even_w_in, gdn_conv, gdn_a_log, gdn_dt_bias, gdn_out_norm, even_w_out, odd_w_qkv, odd_sinks, odd_w_out, peer_w_q, peer_keys, peer_u, peer_v, final_norm):
    b, _, d = x.shape
    depth = ada_w.shape[0]
    rows = 16
    cond = jnp.concatenate([c, c_ctx[None, :], jnp.zeros((rows - b - 1, d), c.dtype)], axis=0)
    mods = ada_mod_all(cond, ada_w, ada_b)
    conv_w_all = gdn_conv
    u_all = peer_u.astype(MXU_DTYPE)
    vt_all = jnp.swapaxes(peer_v.astype(MXU_DTYPE), 1, 2)
    h_c = ctx
    for i in range(depth):
        last = i == depth - 1
        j = i // 2
        m = mods[i].reshape(rows, N_MOD, d)
        m_lat = [m[:b, k] for k in range(N_MOD)]
        m_ctx = [jnp.broadcast_to(m[b, k][None, :], (b, d)) for k in range(N_MOD)]
        if i % 2 == 0:
            x, h_c = even_layer_mixer(x, h_c, m_lat, m_ctx, norm_mix[i], even_w_in[j], conv_w_all[j],
                                      gdn_a_log[j], gdn_dt_bias[j], gdn_out_norm[j], even_w_out[j],
                                      not last)
        else:
            x, h_c = odd_layer_mixer(x, h_c, m_lat, m_ctx, norm_mix[i], odd_w_qkv[j], odd_sinks[j],
                                     odd_w_out[j], not last)
        w_q = peer_w_q[i].astype(MXU_DTYPE)
        keys = peer_keys[i].astype(jnp.float32)
        x = peer_block(x, norm_ffn[i], m_lat[3], m_lat[4], m_lat[5], w_q, keys, u_all, vt_all, i,
                       final_norm if last else None)
        if not last:
            h_c = peer_block(h_c, norm_ffn[i], m_ctx[3], m_ctx[4], m_ctx[5], w_q, keys,
                             u_all, vt_all, i)
    return x
```
